```python
import math
import jax, jax.numpy as jnp
from jax import lax
import numpy as np


D_MODEL = 1024
BATCH = 2
SEQ = 8192
DEPTH = 2

GRID_W = 64
CTX_LEN = 256
N_MOD = 6
FFN_HIDDEN = -(-8 * D_MODEL // (3 * 256)) * 256
NORM_EPS = 1e-6

FOURIER_WIDTH = D_MODEL // 2
FOURIER_GROUPS = 4
FOURIER_GROUP_DIM = FOURIER_WIDTH // FOURIER_GROUPS

SSD_HEAD_DIM = 64
SSD_INNER = 3 * D_MODEL // 2
SSD_HEADS = SSD_INNER // SSD_HEAD_DIM
SSD_GROUPS = 4
SSD_HEADS_PER_GROUP = SSD_HEADS // SSD_GROUPS
SSD_STATE = 128
SSD_CONV = 5
SSD_CHUNK = 128
SSD_XBC = SSD_INNER + 2 * SSD_GROUPS * SSD_STATE
DT_MIN = 0.001
DT_MAX = 0.1
EVEN_IN = FOURIER_WIDTH + SSD_INNER + SSD_XBC + 2 * SSD_HEADS
EVEN_SPLIT = [FOURIER_WIDTH, FOURIER_WIDTH + SSD_INNER, FOURIER_WIDTH + SSD_INNER + SSD_XBC]
EVEN_OUT = FOURIER_WIDTH + SSD_INNER

ATTN_HEADS = D_MODEL // 128
ATTN_QK_DIM = 64
ATTN_V_DIM = 2 * ATTN_QK_DIM
QK_WIDTH = ATTN_HEADS * 2 * ATTN_QK_DIM
ATTN_WIDTH = ATTN_HEADS * ATTN_V_DIM
ATTN_SCALE = ATTN_QK_DIM ** -0.5
Q_BLOCK = 128
ROPE_BASE = 10000.0
ROPE_AXIS_DIM = ATTN_QK_DIM // 2
ROPE_FREQS = ROPE_AXIS_DIM // 2

CONF_CHANNELS = D_MODEL // 2
CONF_WIDTH = 31
ODD_IN = 2 * QK_WIDTH + ATTN_WIDTH + 2 * CONF_CHANNELS
ODD_SPLIT = [QK_WIDTH, 2 * QK_WIDTH, 2 * QK_WIDTH + ATTN_WIDTH]
ODD_OUT = ATTN_WIDTH + CONF_CHANNELS

kernel_name = 'hybrid_fourier_ssd_diffattn_conformer_dit'


def rmsnorm(x, g):
    xf = x.astype(jnp.float32)
    y = xf * lax.rsqrt(jnp.mean(xf * xf, axis=-1, keepdims=True) + NORM_EPS)
    return (y * g.astype(jnp.float32)).astype(x.dtype)


def layernorm(x, g, b):
    xf = x.astype(jnp.float32)
    mu = jnp.mean(xf, axis=-1, keepdims=True)
    var = jnp.mean(jnp.square(xf - mu), axis=-1, keepdims=True)
    y = (xf - mu) * lax.rsqrt(var + NORM_EPS)
    return (y * g.astype(jnp.float32) + b.astype(jnp.float32)).astype(x.dtype)


def modulate(x, shift, scale):
    return x * (1 + scale) + shift


def depthwise_conv(u, w, b):
    k = w.shape[0]
    out = lax.conv_general_dilated(
        u, w[:, None, :].astype(u.dtype), window_strides=(1,),
        padding=[(k // 2, k // 2)], dimension_numbers=('NWC', 'WIO', 'NWC'),
        feature_group_count=u.shape[-1])
    return out + b.astype(u.dtype)


def swiglu(h, wg, wu, wd):
    return (jax.nn.silu(h @ wg) * (h @ wu)) @ wd


def fourier_mix(u):
    b, t, _ = u.shape
    g = u.reshape(b, t, FOURIER_GROUPS, FOURIER_GROUP_DIM).astype(jnp.float32)
    f = jnp.fft.fft2(g, axes=(1, 3), norm='ortho').real
    return f.reshape(b, t, FOURIER_WIDTH).astype(u.dtype)


def ssd_scan(xs, dt, a, bm, cm, state0):
    bsz, t = xs.shape[:2]
    nc = t // SSD_CHUNK

    def chunks(u):
        return u.reshape((bsz, nc, SSD_CHUNK) + u.shape[2:])

    xdt = chunks(xs * dt[..., None])
    bm_c, cm_c = chunks(bm), chunks(cm)
    acs = jnp.cumsum(chunks(dt * a), axis=2)
    mask = jnp.tril(jnp.ones((SSD_CHUNK, SSD_CHUNK), bool))[None, None, :, :, None, None]
    seg = acs[:, :, :, None] - acs[:, :, None, :]
    decay = jnp.exp(jnp.where(mask, seg, -jnp.inf))
    cb = jnp.einsum('bclgn,bcsgn->bclsg', cm_c, bm_c)
    y_diag = jnp.einsum('bclsgh,bcsghp->bclghp', cb[..., None] * decay, xdt)
    to_end = jnp.exp(acs[:, :, -1:] - acs)
    states = jnp.einsum('bcsgn,bcsgh,bcsghp->bcghpn', bm_c, to_end, xdt)
    chunk_decay = jnp.exp(acs[:, :, -1])

    def step(s, inp):
        st, dec = inp
        return s * dec[..., None, None] + st, s

    final, s_in = lax.scan(step, state0, (jnp.moveaxis(states, 1, 0), jnp.moveaxis(chunk_decay, 1, 0)))
    s_in = jnp.moveaxis(s_in, 0, 1)
    y_off = jnp.einsum('bclgn,bcghpn->bclghp', cm_c, s_in) * jnp.exp(acs)[..., None]
    return (y_diag + y_off).reshape(xs.shape), final


def _flip(u, reverse):
    return jnp.flip(u, axis=1) if reverse else u


def ssd_bidirectional(lat, ctx, dt_bias, a_log, d_skip):
    xs_l, bm_l, cm_l, dt_l = lat
    xs_c, bm_c, cm_c, dt_c = ctx
    bsz = xs_l.shape[0]
    gh = (SSD_GROUPS, SSD_HEADS_PER_GROUP)
    ys_l, ys_c = [], []
    for d in range(2):
        rev = d == 1
        a = -jnp.exp(a_log[d].astype(jnp.float32)).reshape(gh)
        bias = dt_bias[d].astype(jnp.float32).reshape(gh)
        dskip = d_skip[d].astype(jnp.float32).reshape(gh)[..., None]
        dl = jax.nn.softplus(dt_l[:, :, d] + bias)
        dc = jax.nn.softplus(dt_c[:, :, d] + bias)
        state0 = jnp.zeros((bsz,) + gh + (SSD_HEAD_DIM, SSD_STATE), jnp.float32)
        yc, sc = ssd_scan(_flip(xs_c, rev), _flip(dc, rev), a, _flip(bm_c, rev), _flip(cm_c, rev), state0)
        yl, _ = ssd_scan(_flip(xs_l, rev), _flip(dl, rev), a, _flip(bm_l, rev), _flip(cm_l, rev), sc)
        ys_c.append(_flip(yc, rev) + dskip * xs_c)
        ys_l.append(_flip(yl, rev) + dskip * xs_l)
    return ys_l[0] + ys_l[1], ys_c[0] + ys_c[1]


def even_mixer(h_lat, h_ctx, w_in, conv_w, conv_b, dt_bias, a_log, d_skip, gnorm_g, w_out, need_ctx):
    def project(h):
        b, t = h.shape[:2]
        f, z, xbc, dt = jnp.split(h @ w_in, EVEN_SPLIT, axis=-1)
        xbc = jax.nn.silu(depthwise_conv(xbc, conv_w, conv_b)).astype(jnp.float32)
        xs, bm, cm = jnp.split(xbc, [SSD_INNER, SSD_INNER + SSD_GROUPS * SSD_STATE], axis=-1)
        ssd_in = (xs.reshape(b, t, SSD_GROUPS, SSD_HEADS_PER_GROUP, SSD_HEAD_DIM),
                  bm.reshape(b, t, SSD_GROUPS, SSD_STATE),
                  cm.reshape(b, t, SSD_GROUPS, SSD_STATE),
                  dt.astype(jnp.float32).reshape(b, t, 2, SSD_GROUPS, SSD_HEADS_PER_GROUP))
        return f, z, ssd_in

    f_l, z_l, ssd_l = project(h_lat)
    f_c, z_c, ssd_c = project(h_ctx)
    y_l, y_c = ssd_bidirectional(ssd_l, ssd_c, dt_bias, a_log, d_skip)

    def finish(f, z, y):
        b, t = z.shape[:2]
        zg = jax.nn.silu(z.astype(jnp.float32)).reshape(y.shape)
        gated = (y * zg).reshape(b, t, SSD_GROUPS, SSD_HEADS_PER_GROUP * SSD_HEAD_DIM)
        yn = rmsnorm(gated, gnorm_g.reshape(SSD_GROUPS, -1)).reshape(b, t, SSD_INNER).astype(z.dtype)
        return jnp.concatenate([fourier_mix(f), yn], axis=-1) @ w_out

    out_c = finish(f_c, z_c, y_c) if need_ctx else None
    return finish(f_l, z_l, y_l), out_c


def apply_axial_rope(u, cos, sin):
    s = u.shape
    u = u.reshape(s[:-1] + (2, 2, ROPE_FREQS))
    cs = cos[None, :, None, None].astype(u.dtype)
    sn = sin[None, :, None, None].astype(u.dtype)
    u1, u2 = u[..., 0, :], u[..., 1, :]
    out = jnp.stack([u1 * cs - u2 * sn, u2 * cs + u1 * sn], axis=-2)
    return out.reshape(s)


def diff_attention_core(q, k, v, lam):
    s = jnp.einsum('bqhcd,bkhcd->bhcqk', q, k).astype(jnp.float32) * ATTN_SCALE
    p = jax.nn.softmax(s, axis=-1)
    w = p[:, :, 0] - lam * p[:, :, 1]
    return jnp.einsum('bhqk,bkhd->bqhd', w.astype(v.dtype), v)


def diff_attention_blocked(q, k, v, lam):
    b, t = q.shape[:2]
    nb = t // Q_BLOCK
    qb = jnp.moveaxis(q.reshape((b, nb, Q_BLOCK) + q.shape[2:]), 1, 0)
    out = lax.map(lambda qi: diff_attention_core(qi, k, v, lam), qb)
    return jnp.moveaxis(out, 0, 1).reshape(b, t, ATTN_HEADS, ATTN_V_DIM)


def conformer_conv(u, conv_w, conv_b, cn_g, cn_b):
    a, gate = jnp.split(u, 2, axis=-1)
    v = a * jax.nn.sigmoid(gate)
    v = depthwise_conv(v, conv_w, conv_b)
    return jax.nn.silu(layernorm(v, cn_g, cn_b))


def odd_mixer(h_lat, h_ctx, cos, sin, w_in, lam, subln_g, conv_w, conv_b, cn_g, cn_b, w_out,
              lambda_init, need_ctx):
    def split_heads(q, k, v):
        b, t = q.shape[:2]
        return (q.reshape(b, t, ATTN_HEADS, 2, ATTN_QK_DIM),
                k.reshape(b, t, ATTN_HEADS, 2, ATTN_QK_DIM),
                v.reshape(b, t, ATTN_HEADS, ATTN_V_DIM))

    q_l, k_l, v_l, u_l = jnp.split(h_lat @ w_in, ODD_SPLIT, axis=-1)
    q_l, k_l, v_l = split_heads(q_l, k_l, v_l)
    q_l = apply_axial_rope(q_l, cos, sin)
    k_l = apply_axial_rope(k_l, cos, sin)
    if need_ctx:
        q_c, k_c, v_c, u_c = jnp.split(h_ctx @ w_in, ODD_SPLIT, axis=-1)
    else:
        q_c = u_c = None
        k_c, v_c = jnp.split(h_ctx @ w_in[:, QK_WIDTH:2 * QK_WIDTH + ATTN_WIDTH], [QK_WIDTH], axis=-1)
        k_c = jnp.concatenate([k_c, k_c[..., :0]], axis=-1)
    b, tc = h_ctx.shape[:2]
    k_c = k_c.reshape(b, tc, ATTN_HEADS, 2, ATTN_QK_DIM)
    v_c = v_c.reshape(b, tc, ATTN_HEADS, ATTN_V_DIM)

    lf = lam.astype(jnp.float32)
    lam_full = jnp.exp(jnp.sum(lf[0] * lf[1])) - jnp.exp(jnp.sum(lf[2] * lf[3])) + lambda_init

    def finish(o, u):
        bb, t = u.shape[:2]
        o = rmsnorm(o, subln_g) * (1 - lambda_init)
        cv = conformer_conv(u, conv_w, conv_b, cn_g, cn_b)
        return jnp.concatenate([o.reshape(bb, t, ATTN_WIDTH), cv], axis=-1) @ w_out

    k_all = jnp.concatenate([k_c, k_l], axis=1)
    v_all = jnp.concatenate([v_c, v_l], axis=1)
    out_l = finish(diff_attention_blocked(q_l, k_all, v_all, lam_full), u_l)
    out_c = None
    if need_ctx:
        q_c = q_c.reshape(b, tc, ATTN_HEADS, 2, ATTN_QK_DIM)
        out_c = finish(diff_attention_core(q_c, k_c, v_c, lam_full), u_c)
    return out_l, out_c


def setup_inputs(seed: int = 0) -> dict:
    key = jax.random.key(seed)
    k = jax.random.split(key, 26)
    f32 = jnp.float32

    def normal(i, shape, scale):
        return jax.random.normal(k[i], shape, f32) * scale

    n_even = (DEPTH + 1) // 2
    n_odd = DEPTH // 2
    dt = jnp.exp(jax.random.uniform(k[13], (n_even, 2, SSD_HEADS), f32, math.log(DT_MIN), math.log(DT_MAX)))
    dt_bias = dt + jnp.log(-jnp.expm1(-dt))
    a_log = jnp.log(jax.random.uniform(k[14], (n_even, 2, SSD_HEADS), f32, 1.0, 16.0))
    return {
        'x': normal(0, (BATCH, SEQ, D_MODEL), 1.0),
        'c': normal(1, (BATCH, D_MODEL), 1.0),
        'ctx': normal(2, (BATCH, CTX_LEN, D_MODEL), 1.0),
        'c_ctx': normal(3, (D_MODEL,), 1.0),
        'mod_w': normal(4, (DEPTH, D_MODEL, N_MOD * D_MODEL), 0.5 * D_MODEL ** -0.5),
        'mod_b': normal(5, (DEPTH, N_MOD * D_MODEL), 0.02),
        'norm_g': 1.0 + normal(6, (DEPTH, 4, D_MODEL), 0.05),
        'ffn_w_gate': normal(7, (DEPTH, D_MODEL, FFN_HIDDEN), D_MODEL ** -0.5),
        'ffn_w_up': normal(8, (DEPTH, D_MODEL, FFN_HIDDEN), D_MODEL ** -0.5),
        'ffn_w_down': normal(9, (DEPTH, FFN_HIDDEN, D_MODEL), FFN_HIDDEN ** -0.5),
        'ev_w_in': normal(10, (n_even, D_MODEL, EVEN_IN), D_MODEL ** -0.5),
        'ev_conv_w': normal(11, (n_even, SSD_CONV, SSD_XBC), SSD_CONV ** -0.5),
        'ev_conv_b': normal(12, (n_even, SSD_XBC), 0.02),
        'ev_dt_bias': dt_bias,
        'ev_a_log': a_log,
        'ev_d_skip': 1.0 + normal(15, (n_even, 2, SSD_HEADS), 0.05),
        'ev_gnorm_g': 1.0 + normal(16, (n_even, SSD_INNER), 0.05),
        'ev_w_out': normal(17, (n_even, EVEN_OUT, D_MODEL), EVEN_OUT ** -0.5),
        'od_w_in': normal(18, (n_odd, D_MODEL, ODD_IN), D_MODEL ** -0.5),
        'od_lambda': normal(19, (n_odd, 4, ATTN_QK_DIM), 0.1),
        'od_subln_g': 1.0 + normal(20, (n_odd, ATTN_V_DIM), 0.05),
        'od_conv_w': normal(21, (n_odd, CONF_WIDTH, CONF_CHANNELS), CONF_WIDTH ** -0.5),
        'od_conv_b': normal(22, (n_odd, CONF_CHANNELS), 0.02),
        'od_cnorm_g': 1.0 + normal(23, (n_odd, CONF_CHANNELS), 0.05),
        'od_cnorm_b': normal(24, (n_odd, CONF_CHANNELS), 0.02),
        'od_w_out': normal(25, (n_odd, ODD_OUT, D_MODEL), ODD_OUT ** -0.5),
    }


def reference(x, c, ctx, c_ctx, mod_w, mod_b, norm_g, ffn_w_gate, ffn_w_up, ffn_w_down,
              ev_w_in, ev_conv_w, ev_conv_b, ev_dt_bias, ev_a_log, ev_d_skip, ev_gnorm_g, ev_w_out,
              od_w_in, od_lambda, od_subln_g, od_conv_w, od_conv_b, od_cnorm_g, od_cnorm_b, od_w_out):
    t = x.shape[1]
    rows = t // GRID_W
    row = jnp.repeat(jnp.arange(rows, dtype=jnp.float32), GRID_W)
    col = jnp.tile(jnp.arange(GRID_W, dtype=jnp.float32), rows)
    inv_freq = ROPE_BASE ** (-jnp.arange(ROPE_FREQS, dtype=jnp.float32) * 2.0 / ROPE_AXIS_DIM)
    ang = jnp.stack([row, col], axis=-1)[:, :, None] * inv_freq
    cos, sin = jnp.cos(ang), jnp.sin(ang)

    h, s = x, ctx
    for i in range(DEPTH):
        last = i == DEPTH - 1
        j = i // 2
        g = norm_g[i]
        m = (jax.nn.silu(c) @ mod_w[i] + mod_b[i]).reshape(-1, N_MOD, 1, D_MODEL)
        mc = (jax.nn.silu(c_ctx) @ mod_w[i] + mod_b[i]).reshape(N_MOD, D_MODEL)
        a_h = modulate(rmsnorm(h, g[0]), m[:, 0], m[:, 1])
        a_s = modulate(rmsnorm(s, g[0]), mc[0], mc[1])
        if i % 2 == 0:
            o_h, o_s = even_mixer(a_h, a_s, ev_w_in[j], ev_conv_w[j], ev_conv_b[j], ev_dt_bias[j],
                                  ev_a_log[j], ev_d_skip[j], ev_gnorm_g[j], ev_w_out[j], not last)
        else:
            lambda_init = 0.8 - 0.6 * math.exp(-0.3 * i)
            o_h, o_s = odd_mixer(a_h, a_s, cos, sin, od_w_in[j], od_lambda[j], od_subln_g[j],
                                 od_conv_w[j], od_conv_b[j], od_cnorm_g[j], od_cnorm_b[j], od_w_out[j],
                                 lambda_init, not last)
        h = h + m[:, 2] * rmsnorm(o_h, g[1])
        f_h = swiglu(modulate(rmsnorm(h, g[2]), m[:, 3], m[:, 4]), ffn_w_gate[i], ffn_w_up[i], ffn_w_down[i])
        h = h + m[:, 5] * rmsnorm(f_h, g[3])
        if not last:
            s = s + mc[2] * rmsnorm(o_s, g[1])
            f_s = swiglu(modulate(rmsnorm(s, g[2]), mc[3], mc[4]), ffn_w_gate[i], ffn_w_up[i], ffn_w_down[i])
            s = s + mc[5] * rmsnorm(f_s, g[3])
    return h
```

```python
import functools
import math

import numpy as np
import jax
import jax.numpy as jnp
from jax import lax
from jax.experimental import pallas as pl
from jax.experimental.pallas import tpu as pltpu

F32 = jnp.float32
BF16 = jnp.bfloat16
HIGHEST = lax.Precision.HIGHEST

D = 1024
B = 2
T = 8192
TC = 256
TT = TC + T
DEPTH = 2
N_MOD = 6
HID = 2816
EPS = 1e-6
GRID_W = 64

TM = 256
NT_CAT = TT // TM
NT_LAT = T // TM

FW = 512
FG = 4
FGD = 128
SSD_INNER = 1536
SSD_G = 4
SSD_HPG = 6
SSD_P = 64
SSD_N = 128
SSD_L = 128
SSD_XBC = 2560
SSD_K = 5
NCH_CAT = TT // SSD_L
NCH_CTX = TC // SSD_L
GW = SSD_HPG * SSD_P
DT_PAD = 128

NH = 8
QK = 64
DV = 128
ATTN_SCALE = QK ** -0.5
CONF_C = 512
CONF_K = 31
ROPE_BASE = 10000.0
TK = 768
NKC = TT // TK

FT1 = 64
FT2 = 128


def _cp(sem, vmem_mb=48):
    return pltpu.CompilerParams(dimension_semantics=sem, vmem_limit_bytes=vmem_mb * 1024 * 1024)


def _sigmoid(x):
    return 1.0 / (1.0 + jnp.exp(-x))


def _silu(x):
    return x * _sigmoid(x)


def _rms(x, g):
    ms = jnp.mean(x * x, axis=-1, keepdims=True)
    return x * lax.rsqrt(ms + EPS) * g


def _dot(a, b):
    return jnp.dot(a, b, preferred_element_type=F32)


def _dot_nt(a, b):
    return lax.dot_general(a, b, (((1,), (1,)), ((), ())), preferred_element_type=F32)


def _split3(x):
    hi = x.astype(BF16)
    r1 = x - hi.astype(F32)
    mid = r1.astype(BF16)
    lo = (r1 - mid.astype(F32)).astype(BF16)
    return hi, mid, lo


def _dot3_l(m01, x):
    hi, mid, lo = _split3(x)
    return _dot(m01, hi) + _dot(m01, mid) + _dot(m01, lo)


def _dot3_r(x, m01):
    hi, mid, lo = _split3(x)
    return _dot(hi, m01) + _dot(mid, m01) + _dot(lo, m01)


def _mod_kernel(c_ref, w_ref, b_ref, o_ref):
    a = _silu(c_ref[...])
    o_ref[...] = jnp.dot(a, w_ref[...], precision=HIGHEST, preferred_element_type=F32) + b_ref[...]


def _mod_vectors(cvec, mod_w, mod_b):
    tn = 1536
    return pl.pallas_call(
        _mod_kernel,
        grid=(DEPTH, N_MOD * D // tn),
        in_specs=[
            pl.BlockSpec((8, D), lambda l, j: (0, 0)),
            pl.BlockSpec((None, D, tn), lambda l, j: (l, 0, j)),
            pl.BlockSpec((None, 1, tn), lambda l, j: (l, 0, j)),
        ],
        out_specs=pl.BlockSpec((None, 8, tn), lambda l, j: (l, 0, j)),
        out_shape=jax.ShapeDtypeStruct((DEPTH, 8, N_MOD * D), F32),
        compiler_params=_cp(("parallel", "parallel")),
        name="mod_vectors",
    )(cvec, mod_w, mod_b.reshape(DEPTH, 1, N_MOD * D))


def _cat_mod_idx(b, i):
    return (b, jnp.minimum(i, 1), 0, 0)


def _lat_mod_idx(b, i):
    return (b, 1, 0, 0)


def _norm_mod_in(s_ref, g_ref, mod_ref, row):
    x = s_ref[...]
    y = _rms(x, g_ref[...])
    return y * (1.0 + mod_ref[row + 1:row + 2, :]) + mod_ref[row:row + 1, :]


def _even_in_kernel(s_ref, g_ref, mod_ref, wf_ref, wz_ref, wx_ref, wd_ref, f_ref, z_ref, x_ref, dt_ref):
    a = _norm_mod_in(s_ref, g_ref, mod_ref, 0).astype(BF16)
    f_ref[...] = _dot(a, wf_ref[...])
    z_ref[...] = _dot(a, wz_ref[...])
    x_ref[...] = _dot(a, wx_ref[...])
    dt_ref[...] = _dot(a, wd_ref[...])


def _even_in(s, g, modtab, wf, wz, wx, wd):
    row = lambda n: pl.BlockSpec((None, TM, n), lambda b, i: (b, i, 0))
    full = lambda a: pl.BlockSpec(a.shape, lambda b, i: (0,) * a.ndim)
    return pl.pallas_call(
        _even_in_kernel,
        grid=(B, NT_CAT),
        in_specs=[row(D), full(g), pl.BlockSpec((None, None, 8, D), _cat_mod_idx),
                  full(wf), full(wz), full(wx), full(wd)],
        out_specs=[row(FW), row(SSD_INNER), row(SSD_XBC), row(DT_PAD)],
        out_shape=[jax.ShapeDtypeStruct((B, TT, n), F32) for n in (FW, SSD_INNER, SSD_XBC, DT_PAD)],
        compiler_params=_cp(("parallel", "parallel"), 56),
        name="even_in_proj",
    )(s, g, modtab, wf, wz, wx, wd)


def _conv5_kernel(x_ref, w_ref, b_ref, o_ref):
    w = w_ref[...]
    bias = b_ref[...]

    def body(c, carry):
        r0 = pl.multiple_of(c * TM, TM)
        center = x_ref[pl.ds(r0, TM), :]
        lo = pl.multiple_of(jnp.maximum(r0 - 8, 0), 8)
        hi = pl.multiple_of(jnp.minimum(r0 + TM, TT - 8), 8)
        left = jnp.where(c >= 2, x_ref[pl.ds(lo, 8), :], 0.0)
        right = jnp.where(jnp.logical_and(c >= 1, c < NT_CAT - 1), x_ref[pl.ds(hi, 8), :], 0.0)
        ext = jnp.concatenate([left, center, right], axis=0)
        acc = bias + w[2:3, :] * center
        for k in (0, 1, 3, 4):
            acc = acc + w[k:k + 1, :] * ext[6 + k:6 + k + TM, :]
        o_ref[pl.ds(r0, TM), :] = _silu(acc)
        return carry

    lax.fori_loop(0, NT_CAT, body, 0)


def _conv5(xbc, w8, bias):
    return pl.pallas_call(
        _conv5_kernel,
        grid=(B, SSD_XBC // 128),
        in_specs=[
            pl.BlockSpec((None, TT, 128), lambda b, c: (b, 0, c)),
            pl.BlockSpec((8, 128), lambda b, c: (0, c)),
            pl.BlockSpec((1, 128), lambda b, c: (0, c)),
        ],
        out_specs=pl.BlockSpec((None, TT, 128), lambda b, c: (b, 0, c)),
        out_shape=jax.ShapeDtypeStruct((B, TT, SSD_XBC), F32),
        compiler_params=_cp(("parallel", "parallel")),
        name="ssd_conv5",
    )(xbc, w8, bias)


def _ssd_chunk_idx(d, i):
    back = jnp.where(i < NCH_CTX, NCH_CTX - 1 - i, NCH_CAT + NCH_CTX - 1 - i)
    return jnp.where(d == 0, i, back)


def _ssd_kernel(xs_ref, bm_ref, cm_ref, dt_ref, bias_ref, alog_ref, y_ref, st_ref):
    d = pl.program_id(1)
    g = pl.program_id(2)
    i = pl.program_id(3)

    @pl.when(i == 0)
    def _():
        st_ref[...] = jnp.zeros_like(st_ref)

    row = lax.broadcasted_iota(jnp.int32, (SSD_L, SSD_L), 0)
    col = lax.broadcasted_iota(jnp.int32, (SSD_L, SSD_L), 1)
    sgn = 1 - 2 * d
    tri = (row - col) * sgn >= 0
    tri_t = (col - row) * sgn >= 0
    tri_b = jnp.where(tri, 1.0, 0.0).astype(BF16)
    base = d * (SSD_G * SSD_HPG) + g * SSD_HPG
    sel = jnp.where(jnp.logical_and(row - col == base, col < SSD_HPG), 1.0, 0.0).astype(BF16)

    raw = dt_ref[...] + bias_ref[...]
    dt_all = jnp.maximum(raw, 0.0) + jnp.log1p(jnp.exp(-jnp.abs(raw)))
    a_all = -jnp.exp(alog_ref[...])
    dt_sel = _dot3_r(dt_all, sel)
    dta_sel = _dot3_r(dt_all * a_all, sel)
    acs = _dot3_l(tri_b, dta_sel)
    tot = jnp.sum(dta_sel, axis=0, keepdims=True)
    to_end = jnp.exp(tot - acs)
    from_start = jnp.exp(acs)
    chunk_decay = jnp.exp(tot)
    acs_t = acs.T

    cm_b = cm_ref[...].astype(BF16)
    bm = bm_ref[...]
    cb = _dot_nt(cm_b, bm.astype(BF16))
    bm_t = bm.T.astype(BF16)

    lane_lo = lax.broadcasted_iota(jnp.int32, (SSD_L, 128), 1) < SSD_P
    lane_lo1 = lax.broadcasted_iota(jnp.int32, (1, 128), 1) < SSD_P

    def head_mat(h):
        seg = acs[:, h:h + 1] - acs_t[h:h + 1, :]
        dec = jnp.exp(jnp.where(tri, seg, -jnp.inf))
        return (cb * dec).astype(BF16)

    for j in range(SSD_HPG // 2):
        h0, h1 = 2 * j, 2 * j + 1
        pair = lambda v: jnp.where(lane_lo, v[:, h0:h0 + 1], v[:, h1:h1 + 1])
        xdt = xs_ref[:, j * 128:(j + 1) * 128] * pair(dt_sel)
        x_lo = jnp.where(lane_lo, xdt, 0.0).astype(BF16)
        x_hi = jnp.where(lane_lo, 0.0, xdt).astype(BF16)
        y_diag = _dot(head_mat(h0), x_lo) + _dot(head_mat(h1), x_hi)
        st_old = st_ref[j]
        y_off = _dot(cm_b, st_old.astype(BF16)) * pair(from_start)
        cd = jnp.where(lane_lo1, chunk_decay[:, h0:h0 + 1], chunk_decay[:, h1:h1 + 1])
        st_ref[j] = st_old * cd + _dot(bm_t, (xdt * pair(to_end)).astype(BF16))
        y_ref[:, j * 128:(j + 1) * 128] = y_diag + y_off


def _ssd(xbc_act, dt, bias_all, alog_all):
    nb = SSD_INNER // 128
    return pl.pallas_call(
        _ssd_kernel,
        grid=(B, 2, SSD_G, NCH_CAT),
        in_specs=[
            pl.BlockSpec((None, SSD_L, GW), lambda b, d, g, i: (b, _ssd_chunk_idx(d, i), g)),
            pl.BlockSpec((None, SSD_L, SSD_N), lambda b, d, g, i: (b, _ssd_chunk_idx(d, i), nb + g)),
            pl.BlockSpec((None, SSD_L, SSD_N), lambda b, d, g, i: (b, _ssd_chunk_idx(d, i), nb + SSD_G + g)),
            pl.BlockSpec((None, SSD_L, DT_PAD), lambda b, d, g, i: (b, _ssd_chunk_idx(d, i), 0)),
            pl.BlockSpec((1, DT_PAD), lambda b, d, g, i: (0, 0)),
            pl.BlockSpec((1, DT_PAD), lambda b, d, g, i: (0, 0)),
        ],
        out_specs=pl.BlockSpec((None, None, SSD_L, GW), lambda b, d, g, i: (d, b, _ssd_chunk_idx(d, i), g)),
        out_shape=jax.ShapeDtypeStruct((2, B, TT, SSD_INNER), F32),
        scratch_shapes=[pltpu.VMEM((SSD_HPG // 2, SSD_N, 128), F32)],
        compiler_params=_cp(("parallel", "parallel", "parallel", "arbitrary")),
        name="ssd_scan",
    )(xbc_act, xbc_act, xbc_act, dt, bias_all, alog_all)


def _dft_cs(n):
    k = np.arange(n, dtype=np.int64)
    ang = 2.0 * np.pi * ((k[:, None] * k[None, :]) % n).astype(np.float64) / n
    return np.cos(ang), np.sin(ang)


def _fourier_tables():
    c1, s1 = _dft_cs(FT1)
    w1 = np.concatenate([c1, -s1], axis=0)
    c2, s2 = _dft_cs(FT2)
    w2 = np.block([[c2, s2], [-s2, c2]])
    cc, sc = _dft_cs(FGD)
    wc = np.concatenate([cc, sc], axis=0)
    k1 = np.arange(FT1, dtype=np.int64)[:, None]
    t2 = np.arange(FT2, dtype=np.int64)[None, :]
    ang = 2.0 * np.pi * ((k1 * t2) % T).astype(np.float64) / T
    tw = np.stack([np.cos(ang), np.sin(ang)], axis=0)[..., None]
    cx, sx = _dft_cs(TC)
    wx = np.concatenate([cx, sx], axis=0)
    wcx = np.concatenate([cc, -sc], axis=0)
    f = lambda a: jnp.asarray(a, dtype=F32)
    return f(w1), f(w2), f(wc), f(tw), f(wx), f(wcx)


def _f1_kernel(w_ref, x_ref, o_ref):
    o_ref[...] = jnp.dot(w_ref[...], x_ref[...], precision=HIGHEST, preferred_element_type=F32)


def _fourier_stage1(w1, xv):
    tn = 8192
    ncol = FT2 * FW
    return pl.pallas_call(
        _f1_kernel,
        grid=(B, ncol // tn),
        in_specs=[pl.BlockSpec((2 * FT1, FT1), lambda b, j: (0, 0)),
                  pl.BlockSpec((None, FT1, tn), lambda b, j: (b, 0, j))],
        out_specs=pl.BlockSpec((None, 2 * FT1, tn), lambda b, j: (b, 0, j)),
        out_shape=jax.ShapeDtypeStruct((B, 2 * FT1, ncol), F32),
        compiler_params=_cp(("parallel", "parallel")),
        name="fourier_stage1",
    )(w1, xv)


def _f2_kernel(y_ref, tw_ref, w2_ref, wc_ref, o_ref):
    yr = y_ref[0]
    yi = y_ref[1]
    c = tw_ref[0]
    s = tw_ref[1]
    y2 = jnp.concatenate([yr * c + yi * s, yi * c - yr * s], axis=0)
    y3 = jnp.dot(w2_ref[...], y2, precision=HIGHEST, preferred_element_type=F32)
    outs = []
    for g in range(FG):
        lhs = jnp.concatenate([y3[:FT2, g * FGD:(g + 1) * FGD], y3[FT2:, g * FGD:(g + 1) * FGD]], axis=1)
        outs.append(jnp.dot(lhs, wc_ref[...], precision=HIGHEST, preferred_element_type=F32))
    o_ref[...] = jnp.concatenate(outs, axis=1) * (1.0 / math.sqrt(T * FGD))


def _fourier_stage2(y1, tw, w2, wc):
    return pl.pallas_call(
        _f2_kernel,
        grid=(B, FT1),
        in_specs=[pl.BlockSpec((None, 2, None, FT2, FW), lambda b, k: (b, 0, k, 0, 0)),
                  pl.BlockSpec((2, None, FT2, 1), lambda b, k: (0, k, 0, 0)),
                  pl.BlockSpec((2 * FT2, 2 * FT2), lambda b, k: (0, 0)),
                  pl.BlockSpec((2 * FGD, FGD), lambda b, k: (0, 0))],
        out_specs=pl.BlockSpec((None, None, FT2, FW), lambda b, k: (b, k, 0, 0)),
        out_shape=jax.ShapeDtypeStruct((B, FT1, FT2, FW), F32),
        compiler_params=_cp(("parallel", "parallel")),
        name="fourier_stage2",
    )(y1, tw, w2, wc)


def _fctx_kernel(x_ref, wx_ref, wc_ref, o_ref):
    ab = jnp.dot(wx_ref[...], x_ref[...], precision=HIGHEST, preferred_element_type=F32)
    outs = []
    for g in range(FG):
        lhs = jnp.concatenate([ab[:TC, g * FGD:(g + 1) * FGD], ab[TC:, g * FGD:(g + 1) * FGD]], axis=1)
        outs.append(jnp.dot(lhs, wc_ref[...], precision=HIGHEST, preferred_element_type=F32))
    o_ref[...] = jnp.concatenate(outs, axis=1) * (1.0 / math.sqrt(TC * FGD))


def _fourier_ctx(f, wx, wcx):
    return pl.pallas_call(
        _fctx_kernel,
        grid=(B,),
        in_specs=[pl.BlockSpec((None, TC, FW), lambda b: (b, 0, 0)),
                  pl.BlockSpec((2 * TC, TC), lambda b: (0, 0)),
                  pl.BlockSpec((2 * FGD, FGD), lambda b: (0, 0))],
        out_specs=pl.BlockSpec((None, TC, FW), lambda b: (b, 0, 0)),
        out_shape=jax.ShapeDtypeStruct((B, TC, FW), F32),
        compiler_params=_cp(("parallel",)),
        name="fourier_ctx",
    )(f, wx, wcx)


def _fourier(f_cat):
    w1, w2, wc, tw, wx, wcx = _fourier_tables()
    f_lat = f_cat[:, TC:, :].reshape(B, FT1, FT2 * FW)
    y1 = _fourier_stage1(w1, f_lat).reshape(B, 2, FT1, FT2, FW)
    y = _fourier_stage2(y1, tw, w2, wc)
    fo_lat = jnp.transpose(y, (0, 2, 1, 3)).reshape(B, T, FW)
    fo_ctx = _fourier_ctx(f_cat, wx, wcx)
    return jnp.concatenate([fo_ctx, fo_lat], axis=1)


def _even_out_kernel(s_ref, fo_ref, y_ref, xs_ref, z_ref, gn_ref, ds_ref, wf_ref, wy_ref, g_ref, mod_ref,
                     o_ref):
    y = y_ref[0] + y_ref[1] + ds_ref[...] * xs_ref[...]
    gated = y * _silu(z_ref[...])
    gn = gn_ref[...]
    parts = []
    for g in range(SSD_G):
        parts.append(_rms(gated[:, g * GW:(g + 1) * GW], gn[:, g * GW:(g + 1) * GW]))
    yn = jnp.concatenate(parts, axis=1).astype(BF16)
    o = _dot(fo_ref[...].astype(BF16), wf_ref[...]) + _dot(yn, wy_ref[...])
    o_ref[...] = s_ref[...] + mod_ref[2:3, :] * _rms(o, g_ref[...])


def _even_out(s, fo, y2, xbc_act, z, gn, dsum, wf, wy, g, modtab):
    row = lambda n: pl.BlockSpec((None, TM, n), lambda b, i: (b, i, 0))
    full = lambda a: pl.BlockSpec(a.shape, lambda b, i: (0,) * a.ndim)
    return pl.pallas_call(
        _even_out_kernel,
        grid=(B, NT_CAT),
        in_specs=[row(D), row(FW),
                  pl.BlockSpec((2, None, TM, SSD_INNER), lambda b, i: (0, b, i, 0)),
                  row(SSD_INNER), row(SSD_INNER), full(gn), full(dsum), full(wf), full(wy), full(g),
                  pl.BlockSpec((None, None, 8, D), _cat_mod_idx)],
        out_specs=row(D),
        out_shape=jax.ShapeDtypeStruct((B, TT, D), F32),
        compiler_params=_cp(("parallel", "parallel"), 56),
        name="even_out_proj",
    )(s, fo, y2, xbc_act, z, gn, dsum, wf, wy, g, modtab)


def _ffn_kernel(s_ref, g2_ref, g3_ref, mod_ref, wg_ref, wu_ref, wd_ref, o_ref):
    a = _norm_mod_in(s_ref, g2_ref, mod_ref, 3).astype(BF16)
    gt = _dot(a, wg_ref[...])
    up = _dot(a, wu_ref[...])
    hid = (_silu(gt) * up).astype(BF16)
    f = _dot(hid, wd_ref[...])
    o_ref[...] = s_ref[...] + mod_ref[5:6, :] * _rms(f, g3_ref[...])


def _ffn(s, g2, g3, modtab, wg, wu, wd, cat):
    nt = NT_CAT if cat else NT_LAT
    rows = TT if cat else T
    row = pl.BlockSpec((None, TM, D), lambda b, i: (b, i, 0))
    full = lambda a: pl.BlockSpec(a.shape, lambda b, i: (0,) * a.ndim)
    return pl.pallas_call(
        _ffn_kernel,
        grid=(B, nt),
        in_specs=[row, full(g2), full(g3),
                  pl.BlockSpec((None, None, 8, D), _cat_mod_idx if cat else _lat_mod_idx),
                  full(wg), full(wu), full(wd)],
        out_specs=row,
        out_shape=jax.ShapeDtypeStruct((B, rows, D), F32),
        compiler_params=_cp(("parallel", "parallel"), 60),
        name="ffn_cat" if cat else "ffn_lat",
    )(s, g2, g3, modtab, wg, wu, wd)


def _rope_tables():
    rows = T // GRID_W
    row = jnp.repeat(jnp.arange(rows, dtype=F32), GRID_W)
    col = jnp.tile(jnp.arange(GRID_W, dtype=F32), rows)
    nf = QK // 4
    inv_freq = ROPE_BASE ** (-jnp.arange(nf, dtype=F32) * 2.0 / (QK // 2))
    ang = jnp.stack([row, col], axis=-1)[:, :, None] * inv_freq
    cos, sin = jnp.cos(ang), jnp.sin(ang)
    lane = np.arange(128)
    axis = (lane % QK) // (QK // 2)
    half = (lane % (QK // 2)) // nf
    freq = lane % nf
    cos_l = cos[:, axis, freq]
    sin_l = sin[:, axis, freq]
    sa = jnp.where(jnp.asarray(half == 0)[None, :], -sin_l, 0.0)
    sb = jnp.where(jnp.asarray(half == 1)[None, :], sin_l, 0.0)
    ctx1 = jnp.ones((TC, 128), F32)
    ctx0 = jnp.zeros((TC, 128), F32)
    return (jnp.concatenate([ctx1, cos_l], axis=0), jnp.concatenate([ctx0, sa], axis=0),
            jnp.concatenate([ctx0, sb], axis=0))


def _odd_in_kernel(s_ref, g_ref, mod_ref, wq_ref, wk_ref, wv_ref, wu_ref, cos_ref, sa_ref, sb_ref,
                   q_ref, k_ref, v_ref, u_ref):
    a = _norm_mod_in(s_ref, g_ref, mod_ref, 0).astype(BF16)
    cos = cos_ref[...]
    sa = sa_ref[...]
    sb = sb_ref[...]

    def rope(x, scale):
        outs = []
        for h in range(NH):
            xh = x[:, h * 128:(h + 1) * 128]
            r = xh * cos + pltpu.roll(xh, 112, 1) * sa + pltpu.roll(xh, 16, 1) * sb
            outs.append(r * scale if scale != 1.0 else r)
        return jnp.concatenate(outs, axis=1)

    q_ref[...] = rope(_dot(a, wq_ref[...]), ATTN_SCALE).astype(BF16)
    k_ref[...] = rope(_dot(a, wk_ref[...]), 1.0).astype(BF16)
    v_ref[...] = _dot(a, wv_ref[...]).astype(BF16)
    u_ref[...] = _dot(a, wu_ref[...])


def _odd_in(s, g, modtab, wq, wk, wv, wu, cos, sa, sb):
    row = lambda n: pl.BlockSpec((None, TM, n), lambda b, i: (b, i, 0))
    full = lambda a: pl.BlockSpec(a.shape, lambda b, i: (0,) * a.ndim)
    tab = pl.BlockSpec((TM, 128), lambda b, i: (i, 0))
    return pl.pallas_call(
        _odd_in_kernel,
        grid=(B, NT_CAT),
        in_specs=[row(D), full(g), pl.BlockSpec((None, None, 8, D), _cat_mod_idx),
                  full(wq), full(wk), full(wv), full(wu), tab, tab, tab],
        out_specs=[row(D), row(D), row(D), row(D)],
        out_shape=[jax.ShapeDtypeStruct((B, TT, D), BF16)] * 3 + [jax.ShapeDtypeStruct((B, TT, D), F32)],
        compiler_params=_cp(("parallel", "parallel"), 56),
        name="odd_in_proj",
    )(s, g, modtab, wq, wk, wv, wu, cos, sa, sb)


def _attn_kernel(q_ref, k_ref, v_ref, lam_ref, sg_ref, o_ref, q2_ref, *, lambda_init):
    q = q_ref[...]
    lane_lo = lax.broadcasted_iota(jnp.int32, (TM, 128), 1) < QK
    zero = jnp.zeros_like(q)
    q2_ref[0:TM, :] = jnp.where(lane_lo, q, zero)
    q2_ref[TM:2 * TM, :] = jnp.where(lane_lo, zero, q)
    q2 = q2_ref[...]

    def body(j, carry):
        m, l, acc = carry
        r0 = pl.multiple_of(j * TK, TK)
        kc = k_ref[pl.ds(r0, TK), :]
        vc = v_ref[pl.ds(r0, TK), :]
        s = _dot_nt(q2, kc)
        m_new = jnp.maximum(m, jnp.max(s, axis=-1, keepdims=True))
        alpha = jnp.exp(m - m_new)
        p = jnp.exp(s - m_new)
        l = alpha * l + jnp.sum(p, axis=-1, keepdims=True)
        acc = alpha * acc + _dot(p.astype(BF16), vc)
        return m_new, l, acc

    init = (jnp.full((2 * TM, 1), -jnp.inf, F32), jnp.zeros((2 * TM, 1), F32), jnp.zeros((2 * TM, DV), F32))
    m, l, acc = lax.fori_loop(0, NKC, body, init)
    o = acc / l
    lf = lam_ref[...]
    lam = (jnp.exp(jnp.sum(lf[0:1, :] * lf[1:2, :], axis=-1, keepdims=True))
           - jnp.exp(jnp.sum(lf[2:3, :] * lf[3:4, :], axis=-1, keepdims=True)) + lambda_init)
    od = o[0:TM, :] - lam * o[TM:2 * TM, :]
    o_ref[...] = _rms(od, sg_ref[...]) * (1.0 - lambda_init)


def _attention(q, k, v, lam, subln_g, lambda_init):
    return pl.pallas_call(
        functools.partial(_attn_kernel, lambda_init=lambda_init),
        grid=(B, NH, NT_LAT),
        in_specs=[pl.BlockSpec((None, TM, 128), lambda b, h, i: (b, i + 1, h)),
                  pl.BlockSpec((None, TT, 128), lambda b, h, i: (b, 0, h)),
                  pl.BlockSpec((None, TT, 128), lambda b, h, i: (b, 0, h)),
                  pl.BlockSpec((4, QK), lambda b, h, i: (0, 0)),
                  pl.BlockSpec((1, DV), lambda b, h, i: (0, 0))],
        out_specs=pl.BlockSpec((None, TM, 128), lambda b, h, i: (b, i, h)),
        out_shape=jax.ShapeDtypeStruct((B, T, NH * DV), F32),
        scratch_shapes=[pltpu.VMEM((2 * TM, 128), BF16)],
        compiler_params=_cp(("parallel", "parallel", "parallel")),
        name="diff_attention",
    )(q, k, v, lam, subln_g)


CONF_HALO = 16


def _conf_conv_kernel(a_ref, gt_ref, w_ref, b_ref, o_ref, vs_ref):
    def fill(c, carry):
        r0 = pl.multiple_of(c * TM, TM)
        v = a_ref[pl.ds(r0, TM), :] * _sigmoid(gt_ref[pl.ds(r0, TM), :])
        vs_ref[pl.ds(r0, TM), :] = jnp.where(c >= 1, v, 0.0)
        return carry

    lax.fori_loop(0, NT_CAT, fill, 0)
    vs_ref[TT:TT + CONF_HALO, :] = jnp.zeros((CONF_HALO, 128), F32)
    w = w_ref[...]
    bias = b_ref[...]

    def body(c, carry):
        start = pl.multiple_of(c * TM + (TC - CONF_HALO), 8)
        win = vs_ref[pl.ds(start, TM + 2 * CONF_HALO), :]
        acc = jnp.broadcast_to(bias, (TM, 128))
        for k in range(CONF_K):
            off = CONF_HALO - CONF_K // 2 + k
            acc = acc + w[k:k + 1, :] * win[off:off + TM, :]
        o_ref[pl.ds(pl.multiple_of(c * TM, TM), TM), :] = acc
        return carry

    lax.fori_loop(0, NT_LAT, body, 0)


def _conf_conv(u, w32, bias):
    nct = CONF_C // 128
    return pl.pallas_call(
        _conf_conv_kernel,
        grid=(B, nct),
        in_specs=[pl.BlockSpec((None, TT, 128), lambda b, c: (b, 0, c)),
                  pl.BlockSpec((None, TT, 128), lambda b, c: (b, 0, nct + c)),
                  pl.BlockSpec((32, 128), lambda b, c: (0, c)),
                  pl.BlockSpec((1, 128), lambda b, c: (0, c))],
        out_specs=pl.BlockSpec((None, T, 128), lambda b, c: (b, 0, c)),
        out_shape=jax.ShapeDtypeStruct((B, T, CONF_C), F32),
        scratch_shapes=[pltpu.VMEM((TT + CONF_HALO, 128), F32)],
        compiler_params=_cp(("parallel", "parallel")),
        name="conformer_conv31",
    )(u, u, w32, bias)


def _odd_out_kernel(s_ref, oa_ref, cv_ref, cg_ref, cb_ref, wa_ref, wc_ref, g_ref, mod_ref, o_ref):
    v = cv_ref[...]
    mu = jnp.mean(v, axis=-1, keepdims=True)
    vc = v - mu
    var = jnp.mean(vc * vc, axis=-1, keepdims=True)
    cv = _silu(vc * lax.rsqrt(var + EPS) * cg_ref[...] + cb_ref[...])
    o = _dot(oa_ref[...].astype(BF16), wa_ref[...]) + _dot(cv.astype(BF16), wc_ref[...])
    o_ref[...] = s_ref[...] + mod_ref[2:3, :] * _rms(o, g_ref[...])


def _odd_out(s_cat, oa, cvpre, cg, cb, wa, wc, g, modtab):
    full = lambda a: pl.BlockSpec(a.shape, lambda b, i: (0,) * a.ndim)
    row = lambda n: pl.BlockSpec((None, TM, n), lambda b, i: (b, i, 0))
    return pl.pallas_call(
        _odd_out_kernel,
        grid=(B, NT_LAT),
        in_specs=[pl.BlockSpec((None, TM, D), lambda b, i: (b, i + 1, 0)), row(NH * DV), row(CONF_C),
                  full(cg), full(cb), full(wa), full(wc), full(g),
                  pl.BlockSpec((None, None, 8, D), _lat_mod_idx)],
        out_specs=row(D),
        out_shape=jax.ShapeDtypeStruct((B, T, D), F32),
        compiler_params=_cp(("parallel", "parallel")),
        name="odd_out_proj",
    )(s_cat, oa, cvpre, cg, cb, wa, wc, g, modtab)


def _pad_rows(a, n):
    return jnp.pad(a, ((0, n - a.shape[0]), (0, 0)))


def _pad_lanes(a, n):
    return jnp.pad(a, ((0, 0), (0, n - a.shape[1])))


def kernel(x, c, ctx, c_ctx, mod_w, mod_b, norm_g, ffn_w_gate, ffn_w_up, ffn_w_down, ev_w_in, ev_conv_w, ev_conv_b, ev_dt_bias, ev_a_log, ev_d_skip, ev_gnorm_g, ev_w_out, od_w_in, od_lambda, od_subln_g, od_conv_w, od_conv_b, od_cnorm_g, od_cnorm_b, od_w_out):
    bf = lambda a: a.astype(BF16)

    cvec = _pad_rows(jnp.concatenate([c, c_ctx[None, :]], axis=0), 8)
    mv = _mod_vectors(cvec, mod_w, mod_b).reshape(DEPTH, 8, N_MOD, D)
    modtabs = []
    for l in range(DEPTH):
        lat = mv[l, :B]
        cx = jnp.broadcast_to(mv[l, B][None], (B, N_MOD, D))
        tab = jnp.stack([cx, lat], axis=1)
        modtabs.append(jnp.pad(tab, ((0, 0), (0, 0), (0, 8 - N_MOD), (0, 0))))

    s = jnp.concatenate([ctx, x], axis=1)

    g = norm_g[0]
    w_in = ev_w_in[0]
    o0, o1, o2 = FW, FW + SSD_INNER, FW + SSD_INNER + SSD_XBC
    f, z, xbc, dt = _even_in(s, g[0:1], modtabs[0], bf(w_in[:, :o0]), bf(w_in[:, o0:o1]),
                             bf(w_in[:, o1:o2]), bf(_pad_lanes(w_in[:, o2:], DT_PAD)))
    xbc_act = _conv5(xbc, _pad_rows(ev_conv_w[0], 8), ev_conv_b[0][None, :])
    bias_all = _pad_lanes(ev_dt_bias[0].reshape(1, -1), DT_PAD)
    alog_all = _pad_lanes(ev_a_log[0].reshape(1, -1), DT_PAD)
    y2 = _ssd(xbc_act, dt, bias_all, alog_all)
    fo = _fourier(f)
    dsum = jnp.repeat(ev_d_skip[0, 0] + ev_d_skip[0, 1], SSD_P)[None, :]
    w_out = ev_w_out[0]
    s = _even_out(s, fo, y2, xbc_act, z, ev_gnorm_g[0][None, :], dsum, bf(w_out[:FW]), bf(w_out[FW:]),
                  g[1:2], modtabs[0])
    s = _ffn(s, g[2:3], g[3:4], modtabs[0], bf(ffn_w_gate[0]), bf(ffn_w_up[0]), bf(ffn_w_down[0]), True)

    g = norm_g[1]
    lambda_init = 0.8 - 0.6 * math.exp(-0.3 * 1)
    w_in = od_w_in[0]
    cos, sa, sb = _rope_tables()
    q, k, v, u = _odd_in(s, g[0:1], modtabs[1], bf(w_in[:, :D]), bf(w_in[:, D:2 * D]),
                         bf(w_in[:, 2 * D:3 * D]), bf(w_in[:, 3 * D:]), cos, sa, sb)
    oa = _attention(q, k, v, od_lambda[0], od_subln_g[0][None, :], lambda_init)
    cvpre = _conf_conv(u, _pad_rows(od_conv_w[0], 32), od_conv_b[0][None, :])
    w_out = od_w_out[0]
    h = _odd_out(s, oa, cvpre, od_cnorm_g[0][None, :], od_cnorm_b[0][None, :], bf(w_out[:NH * DV]),
                 bf(w_out[NH * DV:]), g[1:2], modtabs[1])
    return _ffn(h, g[2:3], g[3:4], modtabs[1], bf(ffn_w_gate[1]), bf(ffn_w_up[1]), bf(ffn_w_down[1]), False)
```

```python
import functools
import math

import numpy as np
import jax
import jax.numpy as jnp
from jax import lax
from jax.experimental import pallas as pl
from jax.experimental.pallas import tpu as pltpu

F32 = jnp.float32
BF16 = jnp.bfloat16
HIGHEST = lax.Precision.HIGHEST

D = 1024
B = 2
T = 8192
TC = 256
TT = TC + T
DEPTH = 2
N_MOD = 6
HID = 2816
EPS = 1e-6
GRID_W = 64

TM = 256
NT_CAT = TT // TM
NT_LAT = T // TM

FW = 512
FG = 4
FGD = 128
SSD_INNER = 1536
SSD_G = 4
SSD_HPG = 6
SSD_P = 64
SSD_N = 128
SSD_L = 128
SSD_XBC = 2560
SSD_K = 5
NCH_CAT = TT // SSD_L
NCH_CTX = TC // SSD_L
GW = SSD_HPG * SSD_P
DT_PAD = 128

NH = 8
QK = 64
DV = 128
ATTN_SCALE = QK ** -0.5
CONF_C = 512
CONF_K = 31
ROPE_BASE = 10000.0
KT = 256
NKT = TT // KT

FT1 = 64
FT2 = 128


def _cp(sem, vmem_mb=48):
    return pltpu.CompilerParams(dimension_semantics=sem, vmem_limit_bytes=vmem_mb * 1024 * 1024)


def _sigmoid(x):
    return 1.0 / (1.0 + jnp.exp(-x))


def _silu(x):
    return x * _sigmoid(x)


def _rms(x, g):
    ms = jnp.mean(x * x, axis=-1, keepdims=True)
    return x * lax.rsqrt(ms + EPS) * g


def _dot(a, b):
    return jnp.dot(a, b, preferred_element_type=F32)


def _dot_nt(a, b):
    return lax.dot_general(a, b, (((1,), (1,)), ((), ())), preferred_element_type=F32)


def _split3(x):
    hi = x.astype(BF16)
    r1 = x - hi.astype(F32)
    mid = r1.astype(BF16)
    lo = (r1 - mid.astype(F32)).astype(BF16)
    return hi, mid, lo


def _dot3_l(m01, x):
    hi, mid, lo = _split3(x)
    return _dot(m01, hi) + _dot(m01, mid) + _dot(m01, lo)


def _dot3_r(x, m01):
    hi, mid, lo = _split3(x)
    return _dot(hi, m01) + _dot(mid, m01) + _dot(lo, m01)


def _mod_kernel(c_ref, w_ref, b_ref, o_ref):
    a = _silu(c_ref[...])
    o_ref[...] = jnp.dot(a, w_ref[...], precision=HIGHEST, preferred_element_type=F32) + b_ref[...]


def _mod_vectors(cvec, mod_w, mod_b):
    tn = 1536
    return pl.pallas_call(
        _mod_kernel,
        grid=(DEPTH, N_MOD * D // tn),
        in_specs=[
            pl.BlockSpec((8, D), lambda l, j: (0, 0)),
            pl.BlockSpec((None, D, tn), lambda l, j: (l, 0, j)),
            pl.BlockSpec((None, 1, tn), lambda l, j: (l, 0, j)),
        ],
        out_specs=pl.BlockSpec((None, 8, tn), lambda l, j: (l, 0, j)),
        out_shape=jax.ShapeDtypeStruct((DEPTH, 8, N_MOD * D), F32),
        compiler_params=_cp(("parallel", "parallel")),
        name="mod_vectors",
    )(cvec, mod_w, mod_b.reshape(DEPTH, 1, N_MOD * D))


def _cat_mod_idx(b, i):
    return (b, jnp.minimum(i, 1), 0, 0)


def _lat_mod_idx(b, i):
    return (b, 1, 0, 0)


def _norm_mod_in(s_ref, g_ref, mod_ref, row):
    x = s_ref[...]
    y = _rms(x, g_ref[...])
    return y * (1.0 + mod_ref[row + 1:row + 2, :]) + mod_ref[row:row + 1, :]


def _even_in_kernel(s_ref, g_ref, mod_ref, wf_ref, wz_ref, wx_ref, wd_ref, f_ref, z_ref, x_ref, dt_ref):
    a = _norm_mod_in(s_ref, g_ref, mod_ref, 0).astype(BF16)
    f_ref[...] = _dot(a, wf_ref[...])
    z_ref[...] = _dot(a, wz_ref[...])
    x_ref[...] = _dot(a, wx_ref[...])
    dt_ref[...] = _dot(a, wd_ref[...])


def _even_in(s, g, modtab, wf, wz, wx, wd):
    row = lambda n: pl.BlockSpec((None, TM, n), lambda b, i: (b, i, 0))
    full = lambda a: pl.BlockSpec(a.shape, lambda b, i: (0,) * a.ndim)
    return pl.pallas_call(
        _even_in_kernel,
        grid=(B, NT_CAT),
        in_specs=[row(D), full(g), pl.BlockSpec((None, None, 8, D), _cat_mod_idx),
                  full(wf), full(wz), full(wx), full(wd)],
        out_specs=[row(FW), row(SSD_INNER), row(SSD_XBC), row(DT_PAD)],
        out_shape=[jax.ShapeDtypeStruct((B, TT, n), F32) for n in (FW, SSD_INNER, SSD_XBC, DT_PAD)],
        compiler_params=_cp(("parallel", "parallel"), 56),
        name="even_in_proj",
    )(s, g, modtab, wf, wz, wx, wd)


def _conv5_kernel(x_ref, w_ref, b_ref, o_ref):
    w = w_ref[...]
    bias = b_ref[...]

    def body(c, carry):
        r0 = pl.multiple_of(c * TM, TM)
        center = x_ref[pl.ds(r0, TM), :]
        lo = pl.multiple_of(jnp.maximum(r0 - 8, 0), 8)
        hi = pl.multiple_of(jnp.minimum(r0 + TM, TT - 8), 8)
        left = jnp.where(c >= 2, x_ref[pl.ds(lo, 8), :], 0.0)
        right = jnp.where(jnp.logical_and(c >= 1, c < NT_CAT - 1), x_ref[pl.ds(hi, 8), :], 0.0)
        ext = jnp.concatenate([left, center, right], axis=0)
        acc = bias + w[2:3, :] * center
        for k in (0, 1, 3, 4):
            acc = acc + w[k:k + 1, :] * ext[6 + k:6 + k + TM, :]
        o_ref[pl.ds(r0, TM), :] = _silu(acc)
        return carry

    lax.fori_loop(0, NT_CAT, body, 0)


def _conv5(xbc, w8, bias):
    return pl.pallas_call(
        _conv5_kernel,
        grid=(B, SSD_XBC // 128),
        in_specs=[
            pl.BlockSpec((None, TT, 128), lambda b, c: (b, 0, c)),
            pl.BlockSpec((8, 128), lambda b, c: (0, c)),
            pl.BlockSpec((1, 128), lambda b, c: (0, c)),
        ],
        out_specs=pl.BlockSpec((None, TT, 128), lambda b, c: (b, 0, c)),
        out_shape=jax.ShapeDtypeStruct((B, TT, SSD_XBC), F32),
        compiler_params=_cp(("parallel", "parallel")),
        name="ssd_conv5",
    )(xbc, w8, bias)


def _ssd_bwd_chunk(i):
    return jnp.where(i < NCH_CTX, NCH_CTX - 1 - i, NCH_CAT + NCH_CTX - 1 - i)


def _ssd_kernel(xf_ref, bf_ref, cf_ref, dtf_ref, xb_ref, bb_ref, cb_ref, dtb_ref, bias_ref, alog_ref,
                yf_ref, yb_ref, st_ref):
    @pl.when(pl.program_id(1) == 0)
    def _():
        st_ref[...] = jnp.zeros_like(st_ref)

    row = lax.broadcasted_iota(jnp.int32, (SSD_L, SSD_L), 0)
    col = lax.broadcasted_iota(jnp.int32, (SSD_L, SSD_L), 1)
    lane_lo = lax.broadcasted_iota(jnp.int32, (SSD_L, 128), 1) < SSD_P
    lane_lo1 = lax.broadcasted_iota(jnp.int32, (1, 128), 1) < SSD_P
    bias = bias_ref[...]
    a_all = -jnp.exp(alog_ref[...])
    dirs = ((xf_ref, bf_ref, cf_ref, dtf_ref, yf_ref), (xb_ref, bb_ref, cb_ref, dtb_ref, yb_ref))

    for d, (xs_ref, bm_ref, cm_ref, dt_ref, y_ref) in enumerate(dirs):
        tri = (row >= col) if d == 0 else (row <= col)
        tri_b = jnp.where(tri, 1.0, 0.0).astype(BF16)
        raw = dt_ref[...] + bias
        dt_all = jnp.maximum(raw, 0.0) + jnp.log1p(jnp.exp(-jnp.abs(raw)))
        dta = dt_all * a_all
        acs = _dot3_l(tri_b, dta)
        tot = jnp.sum(dta, axis=0, keepdims=True)
        to_end = jnp.exp(tot - acs)
        from_start = jnp.exp(acs)
        chunk_decay = jnp.exp(tot)
        acs_t = acs.T

        for g in range(SSD_G):
            cm_b = cm_ref[:, g * SSD_N:(g + 1) * SSD_N].astype(BF16)
            bm = bm_ref[:, g * SSD_N:(g + 1) * SSD_N]
            cb = _dot_nt(cm_b, bm.astype(BF16))
            bm_t = bm.T.astype(BF16)

            def head_mat(c):
                seg = acs[:, c:c + 1] - acs_t[c:c + 1, :]
                dec = jnp.exp(jnp.where(tri, seg, -jnp.inf))
                return (cb * dec).astype(BF16)

            for j in range(SSD_HPG // 2):
                c0 = d * SSD_G * SSD_HPG + g * SSD_HPG + 2 * j
                c1 = c0 + 1
                pair = lambda v: jnp.where(lane_lo, v[:, c0:c0 + 1], v[:, c1:c1 + 1])
                lanes = slice(g * GW + j * 128, g * GW + (j + 1) * 128)
                xdt = xs_ref[:, lanes] * pair(dt_all)
                x_lo = jnp.where(lane_lo, xdt, 0.0).astype(BF16)
                x_hi = jnp.where(lane_lo, 0.0, xdt).astype(BF16)
                y_diag = _dot(head_mat(c0), x_lo) + _dot(head_mat(c1), x_hi)
                st_old = st_ref[d, g, j]
                y_off = _dot(cm_b, st_old.astype(BF16)) * pair(from_start)
                cd = jnp.where(lane_lo1, chunk_decay[:, c0:c0 + 1], chunk_decay[:, c1:c1 + 1])
                st_ref[d, g, j] = st_old * cd + _dot(bm_t, (xdt * pair(to_end)).astype(BF16))
                y_ref[:, lanes] = y_diag + y_off


def _ssd(xbc_act, dt, bias_all, alog_all):
    nbc = SSD_INNER // (SSD_G * SSD_N)
    fwd = lambda blk: (lambda b, i: (b, i, blk))
    bwd = lambda blk: (lambda b, i: (b, _ssd_bwd_chunk(i), blk))
    specs = lambda ix: [pl.BlockSpec((None, SSD_L, SSD_INNER), ix(0)),
                        pl.BlockSpec((None, SSD_L, SSD_G * SSD_N), ix(nbc)),
                        pl.BlockSpec((None, SSD_L, SSD_G * SSD_N), ix(nbc + 1)),
                        pl.BlockSpec((None, SSD_L, DT_PAD), ix(0))]
    par = pl.BlockSpec((1, DT_PAD), lambda b, i: (0, 0))
    return pl.pallas_call(
        _ssd_kernel,
        grid=(B, NCH_CAT),
        in_specs=specs(fwd) + specs(bwd) + [par, par],
        out_specs=[pl.BlockSpec((None, SSD_L, SSD_INNER), fwd(0)),
                   pl.BlockSpec((None, SSD_L, SSD_INNER), bwd(0))],
        out_shape=[jax.ShapeDtypeStruct((B, TT, SSD_INNER), F32)] * 2,
        scratch_shapes=[pltpu.VMEM((2, SSD_G, SSD_HPG // 2, SSD_N, 128), F32)],
        compiler_params=_cp(("parallel", "arbitrary")),
        name="ssd_scan",
    )(xbc_act, xbc_act, xbc_act, dt, xbc_act, xbc_act, xbc_act, dt, bias_all, alog_all)


def _dft_cs(n):
    k = np.arange(n, dtype=np.int64)
    ang = 2.0 * np.pi * ((k[:, None] * k[None, :]) % n).astype(np.float64) / n
    return np.cos(ang), np.sin(ang)


def _fourier_tables():
    c1, s1 = _dft_cs(FT1)
    w1 = np.concatenate([c1, -s1], axis=0)
    c2, s2 = _dft_cs(FT2)
    w2 = np.block([[c2, s2], [-s2, c2]])
    cc, sc = _dft_cs(FGD)
    wc = np.concatenate([cc, sc], axis=0)
    k1 = np.arange(FT1, dtype=np.int64)[:, None]
    t2 = np.arange(FT2, dtype=np.int64)[None, :]
    ang = 2.0 * np.pi * ((k1 * t2) % T).astype(np.float64) / T
    tw = np.stack([np.cos(ang), np.sin(ang)], axis=0)[..., None]
    cx, sx = _dft_cs(TC)
    wx = np.concatenate([cx, sx], axis=0)
    wcx = np.concatenate([cc, -sc], axis=0)
    f = lambda a: jnp.asarray(a, dtype=F32)
    return f(w1), f(w2), f(wc), f(tw), f(wx), f(wcx)


def _f1_kernel(w_ref, x_ref, o_ref):
    o_ref[...] = jnp.dot(w_ref[...], x_ref[...], precision=HIGHEST, preferred_element_type=F32)


def _fourier_stage1(w1, xv):
    tn = 8192
    ncol = FT2 * FW
    return pl.pallas_call(
        _f1_kernel,
        grid=(B, ncol // tn),
        in_specs=[pl.BlockSpec((2 * FT1, FT1), lambda b, j: (0, 0)),
                  pl.BlockSpec((None, FT1, tn), lambda b, j: (b, 0, j))],
        out_specs=pl.BlockSpec((None, 2 * FT1, tn), lambda b, j: (b, 0, j)),
        out_shape=jax.ShapeDtypeStruct((B, 2 * FT1, ncol), F32),
        compiler_params=_cp(("parallel", "parallel")),
        name="fourier_stage1",
    )(w1, xv)


def _f2_kernel(y_ref, tw_ref, w2_ref, wc_ref, o_ref):
    yr = y_ref[0]
    yi = y_ref[1]
    c = tw_ref[0]
    s = tw_ref[1]
    y2 = jnp.concatenate([yr * c + yi * s, yi * c - yr * s], axis=0)
    y3 = jnp.dot(w2_ref[...], y2, precision=HIGHEST, preferred_element_type=F32)
    outs = []
    for g in range(FG):
        lhs = jnp.concatenate([y3[:FT2, g * FGD:(g + 1) * FGD], y3[FT2:, g * FGD:(g + 1) * FGD]], axis=1)
        outs.append(jnp.dot(lhs, wc_ref[...], precision=HIGHEST, preferred_element_type=F32))
    o_ref[...] = jnp.concatenate(outs, axis=1) * (1.0 / math.sqrt(T * FGD))


def _fourier_stage2(y1, tw, w2, wc):
    return pl.pallas_call(
        _f2_kernel,
        grid=(B, FT1),
        in_specs=[pl.BlockSpec((None, 2, None, FT2, FW), lambda b, k: (b, 0, k, 0, 0)),
                  pl.BlockSpec((2, None, FT2, 1), lambda b, k: (0, k, 0, 0)),
                  pl.BlockSpec((2 * FT2, 2 * FT2), lambda b, k: (0, 0)),
                  pl.BlockSpec((2 * FGD, FGD), lambda b, k: (0, 0))],
        out_specs=pl.BlockSpec((None, None, FT2, FW), lambda b, k: (b, k, 0, 0)),
        out_shape=jax.ShapeDtypeStruct((B, FT1, FT2, FW), F32),
        compiler_params=_cp(("parallel", "parallel")),
        name="fourier_stage2",
    )(y1, tw, w2, wc)


def _fctx_kernel(x_ref, wx_ref, wc_ref, o_ref):
    ab = jnp.dot(wx_ref[...], x_ref[...], precision=HIGHEST, preferred_element_type=F32)
    outs = []
    for g in range(FG):
        lhs = jnp.concatenate([ab[:TC, g * FGD:(g + 1) * FGD], ab[TC:, g * FGD:(g + 1) * FGD]], axis=1)
        outs.append(jnp.dot(lhs, wc_ref[...], precision=HIGHEST, preferred_element_type=F32))
    o_ref[...] = jnp.concatenate(outs, axis=1) * (1.0 / math.sqrt(TC * FGD))


def _fourier_ctx(f, wx, wcx):
    return pl.pallas_call(
        _fctx_kernel,
        grid=(B,),
        in_specs=[pl.BlockSpec((None, TC, FW), lambda b: (b, 0, 0)),
                  pl.BlockSpec((2 * TC, TC), lambda b: (0, 0)),
                  pl.BlockSpec((2 * FGD, FGD), lambda b: (0, 0))],
        out_specs=pl.BlockSpec((None, TC, FW), lambda b: (b, 0, 0)),
        out_shape=jax.ShapeDtypeStruct((B, TC, FW), F32),
        compiler_params=_cp(("parallel",)),
        name="fourier_ctx",
    )(f, wx, wcx)


def _fourier(f_cat):
    w1, w2, wc, tw, wx, wcx = _fourier_tables()
    f_lat = f_cat[:, TC:, :].reshape(B, FT1, FT2 * FW)
    y1 = _fourier_stage1(w1, f_lat).reshape(B, 2, FT1, FT2, FW)
    y = _fourier_stage2(y1, tw, w2, wc)
    fo_lat = jnp.transpose(y, (0, 2, 1, 3)).reshape(B, T, FW)
    fo_ctx = _fourier_ctx(f_cat, wx, wcx)
    return jnp.concatenate([fo_ctx, fo_lat], axis=1)


def _even_out_kernel(s_ref, fo_ref, yf_ref, yb_ref, xs_ref, z_ref, gn_ref, ds_ref, wf_ref, wy_ref, g_ref,
                     mod_ref, o_ref):
    y = yf_ref[...] + yb_ref[...] + ds_ref[...] * xs_ref[...]
    gated = y * _silu(z_ref[...])
    gn = gn_ref[...]
    parts = []
    for g in range(SSD_G):
        parts.append(_rms(gated[:, g * GW:(g + 1) * GW], gn[:, g * GW:(g + 1) * GW]))
    yn = jnp.concatenate(parts, axis=1).astype(BF16)
    o = _dot(fo_ref[...].astype(BF16), wf_ref[...]) + _dot(yn, wy_ref[...])
    o_ref[...] = s_ref[...] + mod_ref[2:3, :] * _rms(o, g_ref[...])


def _even_out(s, fo, yf, yb, xbc_act, z, gn, dsum, wf, wy, g, modtab):
    row = lambda n: pl.BlockSpec((None, TM, n), lambda b, i: (b, i, 0))
    full = lambda a: pl.BlockSpec(a.shape, lambda b, i: (0,) * a.ndim)
    return pl.pallas_call(
        _even_out_kernel,
        grid=(B, NT_CAT),
        in_specs=[row(D), row(FW), row(SSD_INNER), row(SSD_INNER),
                  row(SSD_INNER), row(SSD_INNER), full(gn), full(dsum), full(wf), full(wy), full(g),
                  pl.BlockSpec((None, None, 8, D), _cat_mod_idx)],
        out_specs=row(D),
        out_shape=jax.ShapeDtypeStruct((B, TT, D), F32),
        compiler_params=_cp(("parallel", "parallel"), 56),
        name="even_out_proj",
    )(s, fo, yf, yb, xbc_act, z, gn, dsum, wf, wy, g, modtab)


def _ffn_kernel(s_ref, g2_ref, g3_ref, mod_ref, wg_ref, wu_ref, wd_ref, o_ref):
    a = _norm_mod_in(s_ref, g2_ref, mod_ref, 3).astype(BF16)
    gt = _dot(a, wg_ref[...])
    up = _dot(a, wu_ref[...])
    hid = (_silu(gt) * up).astype(BF16)
    f = _dot(hid, wd_ref[...])
    o_ref[...] = s_ref[...] + mod_ref[5:6, :] * _rms(f, g3_ref[...])


def _ffn(s, g2, g3, modtab, wg, wu, wd, cat):
    nt = NT_CAT if cat else NT_LAT
    rows = TT if cat else T
    row = pl.BlockSpec((None, TM, D), lambda b, i: (b, i, 0))
    full = lambda a: pl.BlockSpec(a.shape, lambda b, i: (0,) * a.ndim)
    return pl.pallas_call(
        _ffn_kernel,
        grid=(B, nt),
        in_specs=[row, full(g2), full(g3),
                  pl.BlockSpec((None, None, 8, D), _cat_mod_idx if cat else _lat_mod_idx),
                  full(wg), full(wu), full(wd)],
        out_specs=row,
        out_shape=jax.ShapeDtypeStruct((B, rows, D), F32),
        compiler_params=_cp(("parallel", "parallel"), 60),
        name="ffn_cat" if cat else "ffn_lat",
    )(s, g2, g3, modtab, wg, wu, wd)


def _rope_tables():
    rows = T // GRID_W
    row = jnp.repeat(jnp.arange(rows, dtype=F32), GRID_W)
    col = jnp.tile(jnp.arange(GRID_W, dtype=F32), rows)
    nf = QK // 4
    inv_freq = ROPE_BASE ** (-jnp.arange(nf, dtype=F32) * 2.0 / (QK // 2))
    ang = jnp.stack([row, col], axis=-1)[:, :, None] * inv_freq
    cos, sin = jnp.cos(ang), jnp.sin(ang)
    lane = np.arange(128)
    axis = (lane % QK) // (QK // 2)
    half = (lane % (QK // 2)) // nf
    freq = lane % nf
    cos_l = cos[:, axis, freq]
    sin_l = sin[:, axis, freq]
    sa = jnp.where(jnp.asarray(half == 0)[None, :], -sin_l, 0.0)
    sb = jnp.where(jnp.asarray(half == 1)[None, :], sin_l, 0.0)
    ctx1 = jnp.ones((TC, 128), F32)
    ctx0 = jnp.zeros((TC, 128), F32)
    return (jnp.concatenate([ctx1, cos_l], axis=0), jnp.concatenate([ctx0, sa], axis=0),
            jnp.concatenate([ctx0, sb], axis=0))


def _odd_in_kernel(s_ref, g_ref, mod_ref, wq_ref, wk_ref, wv_ref, wu_ref, cos_ref, sa_ref, sb_ref,
                   q_ref, k_ref, v_ref, u_ref):
    a = _norm_mod_in(s_ref, g_ref, mod_ref, 0).astype(BF16)
    cos = cos_ref[...]
    sa = sa_ref[...]
    sb = sb_ref[...]

    def rope(x, scale):
        outs = []
        for h in range(NH):
            xh = x[:, h * 128:(h + 1) * 128]
            r = xh * cos + pltpu.roll(xh, 112, 1) * sa + pltpu.roll(xh, 16, 1) * sb
            outs.append(r * scale if scale != 1.0 else r)
        return jnp.concatenate(outs, axis=1)

    q_ref[...] = rope(_dot(a, wq_ref[...]), ATTN_SCALE * math.log2(math.e)).astype(BF16)
    k_ref[...] = rope(_dot(a, wk_ref[...]), 1.0).astype(BF16)
    v_ref[...] = _dot(a, wv_ref[...]).astype(BF16)
    u_ref[...] = _dot(a, wu_ref[...])


def _odd_in(s, g, modtab, wq, wk, wv, wu, cos, sa, sb):
    row = lambda n: pl.BlockSpec((None, TM, n), lambda b, i: (b, i, 0))
    full = lambda a: pl.BlockSpec(a.shape, lambda b, i: (0,) * a.ndim)
    tab = pl.BlockSpec((TM, 128), lambda b, i: (i, 0))
    return pl.pallas_call(
        _odd_in_kernel,
        grid=(B, NT_CAT),
        in_specs=[row(D), full(g), pl.BlockSpec((None, None, 8, D), _cat_mod_idx),
                  full(wq), full(wk), full(wv), full(wu), tab, tab, tab],
        out_specs=[row(D), row(D), row(D), row(D)],
        out_shape=[jax.ShapeDtypeStruct((B, TT, D), BF16)] * 3 + [jax.ShapeDtypeStruct((B, TT, D), F32)],
        compiler_params=_cp(("parallel", "parallel"), 56),
        name="odd_in_proj",
    )(s, g, modtab, wq, wk, wv, wu, cos, sa, sb)


def _attn_kernel(qa_ref, qb_ref, qn_ref, k_ref, v_ref, lam_ref, sg_ref, o_ref,
                 q2_ref, s_ref, mp_ref, mb_ref, ls_ref, acc_ref, *, lambda_init):
    i = pl.program_id(2)
    lane_lo = lax.broadcasted_iota(jnp.int32, (TM, 128), 1) < QK

    def stack(src_ref, slot):
        q = src_ref[...]
        zero = jnp.zeros_like(q)
        q2_ref[slot, 0:TM, :] = jnp.where(lane_lo, q, zero)
        q2_ref[slot, TM:2 * TM, :] = jnp.where(lane_lo, zero, q)

    stack(qa_ref, 0)
    stack(qb_ref, 1)
    stack(qn_ref, 2)

    def produce(c, qslot, sslot):
        r0 = pl.multiple_of(c * KT, KT)
        s = _dot_nt(q2_ref[qslot], k_ref[pl.ds(r0, KT), :])
        s_ref[sslot, c] = s
        mp_ref[...] = jnp.maximum(mp_ref[...], jnp.maximum(s[:, 0:128], s[:, 128:256]))

    def finish_max():
        m = jnp.max(mp_ref[...], axis=-1, keepdims=True)
        mb_ref[...] = jnp.broadcast_to(m, mb_ref.shape)
        mp_ref[...] = jnp.full(mp_ref.shape, -jnp.inf, F32)

    def consume(c, sslot):
        r0 = pl.multiple_of(c * KT, KT)
        s = s_ref[sslot, c]
        mb = mb_ref[...]
        p0 = jnp.exp2(s[:, 0:128] - mb)
        p1 = jnp.exp2(s[:, 128:256] - mb)
        ls_ref[...] += p0 + p1
        p = jnp.concatenate([p0, p1], axis=1).astype(BF16)
        acc_ref[...] += _dot(p, v_ref[pl.ds(r0, KT), :])

    lf = lam_ref[...]
    lam = (jnp.exp(jnp.sum(lf[0:1, :] * lf[1:2, :], axis=-1, keepdims=True))
           - jnp.exp(jnp.sum(lf[2:3, :] * lf[3:4, :], axis=-1, keepdims=True)) + lambda_init)

    def phase(cur_slot, next_q, next_slot, out_rows):
        ls_ref[...] = jnp.zeros(ls_ref.shape, F32)
        acc_ref[...] = jnp.zeros(acc_ref.shape, F32)

        def body(c, carry):
            produce(c, next_q, next_slot)
            consume(c, cur_slot)
            return carry

        lax.fori_loop(0, NKT, body, 0, unroll=11)
        finish_max()
        o = acc_ref[...] / jnp.sum(ls_ref[...], axis=-1, keepdims=True)
        od = o[0:TM, :] - lam * o[TM:2 * TM, :]
        o_ref[out_rows, :] = _rms(od, sg_ref[...]) * (1.0 - lambda_init)

    @pl.when(i == 0)
    def _():
        mp_ref[...] = jnp.full(mp_ref.shape, -jnp.inf, F32)

        def body(c, carry):
            produce(c, 0, 0)
            return carry

        lax.fori_loop(0, NKT, body, 0)
        finish_max()

    phase(0, 1, 1, slice(0, TM))
    phase(1, 2, 0, slice(TM, 2 * TM))


def _attention(q, k, v, lam, subln_g, lambda_init):
    nsteps = NT_LAT // 2
    return pl.pallas_call(
        functools.partial(_attn_kernel, lambda_init=lambda_init),
        grid=(B, NH, nsteps),
        in_specs=[pl.BlockSpec((None, TM, 128), lambda b, h, i: (b, 2 * i + 1, h)),
                  pl.BlockSpec((None, TM, 128), lambda b, h, i: (b, 2 * i + 2, h)),
                  pl.BlockSpec((None, TM, 128), lambda b, h, i: (b, jnp.minimum(2 * i + 3, NT_LAT), h)),
                  pl.BlockSpec((None, TT, 128), lambda b, h, i: (b, 0, h)),
                  pl.BlockSpec((None, TT, 128), lambda b, h, i: (b, 0, h)),
                  pl.BlockSpec((4, QK), lambda b, h, i: (0, 0)),
                  pl.BlockSpec((1, DV), lambda b, h, i: (0, 0))],
        out_specs=pl.BlockSpec((None, 2 * TM, 128), lambda b, h, i: (b, i, h)),
        out_shape=jax.ShapeDtypeStruct((B, T, NH * DV), F32),
        scratch_shapes=[pltpu.VMEM((3, 2 * TM, 128), BF16),
                        pltpu.VMEM((2, NKT, 2 * TM, KT), F32),
                        pltpu.VMEM((2 * TM, 128), F32),
                        pltpu.VMEM((2 * TM, 128), F32),
                        pltpu.VMEM((2 * TM, 128), F32),
                        pltpu.VMEM((2 * TM, DV), F32)],
        compiler_params=_cp(("parallel", "parallel", "arbitrary"), 58),
        name="diff_attention",
    )(q, q, q, k, v, lam, subln_g)


CONF_HALO = 16


def _conf_conv_kernel(a_ref, gt_ref, w_ref, b_ref, o_ref, vs_ref):
    def fill(c, carry):
        r0 = pl.multiple_of(c * TM, TM)
        v = a_ref[pl.ds(r0, TM), :] * _sigmoid(gt_ref[pl.ds(r0, TM), :])
        vs_ref[pl.ds(r0, TM), :] = jnp.where(c >= 1, v, 0.0)
        return carry

    lax.fori_loop(0, NT_CAT, fill, 0)
    vs_ref[TT:TT + CONF_HALO, :] = jnp.zeros((CONF_HALO, 128), F32)
    w = w_ref[...]
    bias = b_ref[...]

    def body(c, carry):
        start = pl.multiple_of(c * TM + (TC - CONF_HALO), 8)
        win = vs_ref[pl.ds(start, TM + 2 * CONF_HALO), :]
        acc = jnp.broadcast_to(bias, (TM, 128))
        for k in range(CONF_K):
            off = CONF_HALO - CONF_K // 2 + k
            acc = acc + w[k:k + 1, :] * win[off:off + TM, :]
        o_ref[pl.ds(pl.multiple_of(c * TM, TM), TM), :] = acc
        return carry

    lax.fori_loop(0, NT_LAT, body, 0)


def _conf_conv(u, w32, bias):
    nct = CONF_C // 128
    return pl.pallas_call(
        _conf_conv_kernel,
        grid=(B, nct),
        in_specs=[pl.BlockSpec((None, TT, 128), lambda b, c: (b, 0, c)),
                  pl.BlockSpec((None, TT, 128), lambda b, c: (b, 0, nct + c)),
                  pl.BlockSpec((32, 128), lambda b, c: (0, c)),
                  pl.BlockSpec((1, 128), lambda b, c: (0, c))],
        out_specs=pl.BlockSpec((None, T, 128), lambda b, c: (b, 0, c)),
        out_shape=jax.ShapeDtypeStruct((B, T, CONF_C), F32),
        scratch_shapes=[pltpu.VMEM((TT + CONF_HALO, 128), F32)],
        compiler_params=_cp(("parallel", "parallel")),
        name="conformer_conv31",
    )(u, u, w32, bias)


def _odd_out_kernel(s_ref, oa_ref, cv_ref, cg_ref, cb_ref, wa_ref, wc_ref, g_ref, mod_ref, o_ref):
    v = cv_ref[...]
    mu = jnp.mean(v, axis=-1, keepdims=True)
    vc = v - mu
    var = jnp.mean(vc * vc, axis=-1, keepdims=True)
    cv = _silu(vc * lax.rsqrt(var + EPS) * cg_ref[...] + cb_ref[...])
    o = _dot(oa_ref[...].astype(BF16), wa_ref[...]) + _dot(cv.astype(BF16), wc_ref[...])
    o_ref[...] = s_ref[...] + mod_ref[2:3, :] * _rms(o, g_ref[...])


def _odd_out(s_cat, oa, cvpre, cg, cb, wa, wc, g, modtab):
    full = lambda a: pl.BlockSpec(a.shape, lambda b, i: (0,) * a.ndim)
    row = lambda n: pl.BlockSpec((None, TM, n), lambda b, i: (b, i, 0))
    return pl.pallas_call(
        _odd_out_kernel,
        grid=(B, NT_LAT),
        in_specs=[pl.BlockSpec((None, TM, D), lambda b, i: (b, i + 1, 0)), row(NH * DV), row(CONF_C),
                  full(cg), full(cb), full(wa), full(wc), full(g),
                  pl.BlockSpec((None, None, 8, D), _lat_mod_idx)],
        out_specs=row(D),
        out_shape=jax.ShapeDtypeStruct((B, T, D), F32),
        compiler_params=_cp(("parallel", "parallel")),
        name="odd_out_proj",
    )(s_cat, oa, cvpre, cg, cb, wa, wc, g, modtab)


def _pad_rows(a, n):
    return jnp.pad(a, ((0, n - a.shape[0]), (0, 0)))


def _pad_lanes(a, n):
    return jnp.pad(a, ((0, 0), (0, n - a.shape[1])))


def kernel(x, c, ctx, c_ctx, mod_w, mod_b, norm_g, ffn_w_gate, ffn_w_up, ffn_w_down, ev_w_in, ev_conv_w, ev_conv_b, ev_dt_bias, ev_a_log, ev_d_skip, ev_gnorm_g, ev_w_out, od_w_in, od_lambda, od_subln_g, od_conv_w, od_conv_b, od_cnorm_g, od_cnorm_b, od_w_out):
    bf = lambda a: a.astype(BF16)

    cvec = _pad_rows(jnp.concatenate([c, c_ctx[None, :]], axis=0), 8)
    mv = _mod_vectors(cvec, mod_w, mod_b).reshape(DEPTH, 8, N_MOD, D)
    modtabs = []
    for l in range(DEPTH):
        lat = mv[l, :B]
        cx = jnp.broadcast_to(mv[l, B][None], (B, N_MOD, D))
        tab = jnp.stack([cx, lat], axis=1)
        modtabs.append(jnp.pad(tab, ((0, 0), (0, 0), (0, 8 - N_MOD), (0, 0))))

    s = jnp.concatenate([ctx, x], axis=1)

    g = norm_g[0]
    w_in = ev_w_in[0]
    o0, o1, o2 = FW, FW + SSD_INNER, FW + SSD_INNER + SSD_XBC
    f, z, xbc, dt = _even_in(s, g[0:1], modtabs[0], bf(w_in[:, :o0]), bf(w_in[:, o0:o1]),
                             bf(w_in[:, o1:o2]), bf(_pad_lanes(w_in[:, o2:], DT_PAD)))
    xbc_act = _conv5(xbc, _pad_rows(ev_conv_w[0], 8), ev_conv_b[0][None, :])
    bias_all = _pad_lanes(ev_dt_bias[0].reshape(1, -1), DT_PAD)
    alog_all = _pad_lanes(ev_a_log[0].reshape(1, -1), DT_PAD)
    yf, yb = _ssd(xbc_act, dt, bias_all, alog_all)
    fo = _fourier(f)
    dsum = jnp.repeat(ev_d_skip[0, 0] + ev_d_skip[0, 1], SSD_P)[None, :]
    w_out = ev_w_out[0]
    s = _even_out(s, fo, yf, yb, xbc_act, z, ev_gnorm_g[0][None, :], dsum, bf(w_out[:FW]), bf(w_out[FW:]),
                  g[1:2], modtabs[0])
    s = _ffn(s, g[2:3], g[3:4], modtabs[0], bf(ffn_w_gate[0]), bf(ffn_w_up[0]), bf(ffn_w_down[0]), True)

    g = norm_g[1]
    lambda_init = 0.8 - 0.6 * math.exp(-0.3 * 1)
    w_in = od_w_in[0]
    cos, sa, sb = _rope_tables()
    q, k, v, u = _odd_in(s, g[0:1], modtabs[1], bf(w_in[:, :D]), bf(w_in[:, D:2 * D]),
                         bf(w_in[:, 2 * D:3 * D]), bf(w_in[:, 3 * D:]), cos, sa, sb)
    oa = _attention(q, k, v, od_lambda[0], od_subln_g[0][None, :], lambda_init)
    cvpre = _conf_conv(u, _pad_rows(od_conv_w[0], 32), od_conv_b[0][None, :])
    w_out = od_w_out[0]
    h = _odd_out(s, oa, cvpre, od_cnorm_g[0][None, :], od_cnorm_b[0][None, :], bf(w_out[:NH * DV]),
                 bf(w_out[NH * DV:]), g[1:2], modtabs[1])
    return _ffn(h, g[2:3], g[3:4], modtabs[1], bf(ffn_w_gate[1]), bf(ffn_w_up[1]), bf(ffn_w_down[1]), False)
```

```python
import functools
import math

import numpy as np
import jax
import jax.numpy as jnp
from jax import lax
from jax.experimental import pallas as pl
from jax.experimental.pallas import tpu as pltpu

F32 = jnp.float32
BF16 = jnp.bfloat16
HIGHEST = lax.Precision.HIGHEST

D = 1024
B = 2
T = 8192
TC = 256
TT = TC + T
DEPTH = 2
N_MOD = 6
HID = 2816
EPS = 1e-6
GRID_W = 64

TM = 256
NT_CAT = TT // TM
NT_LAT = T // TM

FW = 512
FG = 4
FGD = 128
SSD_INNER = 1536
SSD_G = 4
SSD_HPG = 6
SSD_P = 64
SSD_N = 128
SSD_L = 128
SSD_XBC = 2560
SSD_K = 5
NCH_CAT = TT // SSD_L
NCH_CTX = TC // SSD_L
GW = SSD_HPG * SSD_P
DT_PAD = 128

NH = 8
QK = 64
DV = 128
ATTN_SCALE = QK ** -0.5
CONF_C = 512
CONF_K = 31
ROPE_BASE = 10000.0
KT = 256
NKT = TT // KT

FT1 = 64
FT2 = 128


def _cp(sem, vmem_mb=48):
    return pltpu.CompilerParams(dimension_semantics=sem, vmem_limit_bytes=vmem_mb * 1024 * 1024)


def _sigmoid(x):
    return 1.0 / (1.0 + jnp.exp(-x))


def _silu(x):
    return x * _sigmoid(x)


def _rms(x, g):
    ms = jnp.mean(x * x, axis=-1, keepdims=True)
    return x * lax.rsqrt(ms + EPS) * g


def _dot(a, b):
    return jnp.dot(a, b, preferred_element_type=F32)


def _dot_nt(a, b):
    return lax.dot_general(a, b, (((1,), (1,)), ((), ())), preferred_element_type=F32)


def _split3(x):
    hi = x.astype(BF16)
    r1 = x - hi.astype(F32)
    mid = r1.astype(BF16)
    lo = (r1 - mid.astype(F32)).astype(BF16)
    return hi, mid, lo


def _dot3_l(m01, x):
    hi, mid, lo = _split3(x)
    return _dot(m01, hi) + _dot(m01, mid) + _dot(m01, lo)


def _dot3_r(x, m01):
    hi, mid, lo = _split3(x)
    return _dot(hi, m01) + _dot(mid, m01) + _dot(lo, m01)


def _mod_kernel(c_ref, w_ref, b_ref, o_ref):
    a = _silu(c_ref[...])
    o_ref[...] = jnp.dot(a, w_ref[...], precision=HIGHEST, preferred_element_type=F32) + b_ref[...]


def _mod_vectors(cvec, mod_w, mod_b):
    tn = 1536
    return pl.pallas_call(
        _mod_kernel,
        grid=(DEPTH, N_MOD * D // tn),
        in_specs=[
            pl.BlockSpec((8, D), lambda l, j: (0, 0)),
            pl.BlockSpec((None, D, tn), lambda l, j: (l, 0, j)),
            pl.BlockSpec((None, 1, tn), lambda l, j: (l, 0, j)),
        ],
        out_specs=pl.BlockSpec((None, 8, tn), lambda l, j: (l, 0, j)),
        out_shape=jax.ShapeDtypeStruct((DEPTH, 8, N_MOD * D), F32),
        compiler_params=_cp(("parallel", "parallel")),
        name="mod_vectors",
    )(cvec, mod_w, mod_b.reshape(DEPTH, 1, N_MOD * D))


def _cat_mod_idx(b, i):
    return (b, jnp.minimum(i, 1), 0, 0)


def _lat_mod_idx(b, i):
    return (b, 1, 0, 0)


def _norm_mod_in(s_ref, g_ref, mod_ref, row):
    x = s_ref[...]
    y = _rms(x, g_ref[...])
    return y * (1.0 + mod_ref[row + 1:row + 2, :]) + mod_ref[row:row + 1, :]


def _even_in_kernel(s_ref, g_ref, mod_ref, wf_ref, wz_ref, wx_ref, wd_ref, f_ref, z_ref, x_ref, dt_ref):
    a = _norm_mod_in(s_ref, g_ref, mod_ref, 0).astype(BF16)
    f_ref[...] = _dot(a, wf_ref[...])
    z_ref[...] = _dot(a, wz_ref[...]).astype(BF16)
    x_ref[...] = _dot(a, wx_ref[...]).astype(BF16)
    dt_ref[...] = _dot(a, wd_ref[...])


def _even_in(s, g, modtab, wf, wz, wx, wd):
    row = lambda n: pl.BlockSpec((None, TM, n), lambda b, i: (b, i, 0))
    full = lambda a: pl.BlockSpec(a.shape, lambda b, i: (0,) * a.ndim)
    return pl.pallas_call(
        _even_in_kernel,
        grid=(B, NT_CAT),
        in_specs=[row(D), full(g), pl.BlockSpec((None, None, 8, D), _cat_mod_idx),
                  full(wf), full(wz), full(wx), full(wd)],
        out_specs=[row(FW), row(SSD_INNER), row(SSD_XBC), row(DT_PAD)],
        out_shape=[jax.ShapeDtypeStruct((B, TT, n), dt) for n, dt in
                   ((FW, F32), (SSD_INNER, BF16), (SSD_XBC, BF16), (DT_PAD, F32))],
        compiler_params=_cp(("parallel", "parallel"), 56),
        name="even_in_proj",
    )(s, g, modtab, wf, wz, wx, wd)


CONV_HALO = 16


def _conv5_kernel(x_ref, w_ref, b_ref, o_ref):
    w = w_ref[...]
    bias = b_ref[...]

    def body(c, carry):
        r0 = pl.multiple_of(c * TM, TM)
        center = x_ref[pl.ds(r0, TM), :].astype(F32)
        lo = pl.multiple_of(jnp.maximum(r0 - CONV_HALO, 0), CONV_HALO)
        hi = pl.multiple_of(jnp.minimum(r0 + TM, TT - CONV_HALO), CONV_HALO)
        left = jnp.where(c >= 2, x_ref[pl.ds(lo, CONV_HALO), :].astype(F32), 0.0)
        right = jnp.where(jnp.logical_and(c >= 1, c < NT_CAT - 1),
                          x_ref[pl.ds(hi, CONV_HALO), :].astype(F32), 0.0)
        ext = jnp.concatenate([left, center, right], axis=0)
        acc = bias + w[2:3, :] * center
        for k in (0, 1, 3, 4):
            off = CONV_HALO - SSD_K // 2 + k
            acc = acc + w[k:k + 1, :] * ext[off:off + TM, :]
        o_ref[pl.ds(r0, TM), :] = _silu(acc).astype(BF16)
        return carry

    lax.fori_loop(0, NT_CAT, body, 0)


def _conv5(xbc, w8, bias):
    return pl.pallas_call(
        _conv5_kernel,
        grid=(B, SSD_XBC // 128),
        in_specs=[
            pl.BlockSpec((None, TT, 128), lambda b, c: (b, 0, c)),
            pl.BlockSpec((8, 128), lambda b, c: (0, c)),
            pl.BlockSpec((1, 128), lambda b, c: (0, c)),
        ],
        out_specs=pl.BlockSpec((None, TT, 128), lambda b, c: (b, 0, c)),
        out_shape=jax.ShapeDtypeStruct((B, TT, SSD_XBC), BF16),
        compiler_params=_cp(("parallel", "parallel")),
        name="ssd_conv5",
    )(xbc, w8, bias)


def _ssd_bwd_chunk(i):
    return jnp.where(i < NCH_CTX, NCH_CTX - 1 - i, NCH_CAT + NCH_CTX - 1 - i)


SSD_NH = SSD_G * SSD_HPG


def _ssd_expand_tables():
    lane = np.arange(DT_PAD)[:, None]
    col = np.arange(SSD_INNER)[None, :]
    colf = np.arange(SSD_NH * 128)[None, :]
    pair, full = [], []
    for d in range(2):
        head_of_col = d * SSD_NH + 2 * (col // 128) + (col % 128) // SSD_P
        e = (lane == head_of_col).astype(np.float32)
        pair.append(np.concatenate([e, e], axis=0))
        ef = (lane == d * SSD_NH + colf // 128).astype(np.float32)
        full.append(np.concatenate([ef, ef], axis=0))
    return jnp.asarray(np.stack(pair), dtype=BF16), jnp.asarray(np.stack(full), dtype=BF16)


def _hi_mid(x):
    hi = x.astype(BF16)
    mid = (x - hi.astype(F32)).astype(BF16)
    return jnp.concatenate([hi, mid], axis=1)


def _ssd_kernel(xf_ref, bf_ref, cf_ref, dtf_ref, xb_ref, bb_ref, cb_ref, dtb_ref, bias_ref, alog_ref,
                ep_ref, ef_ref, yf_ref, yb_ref, st_ref):
    @pl.when(pl.program_id(1) == 0)
    def _():
        st_ref[...] = jnp.zeros_like(st_ref)

    row = lax.broadcasted_iota(jnp.int32, (SSD_L, SSD_L), 0)
    col = lax.broadcasted_iota(jnp.int32, (SSD_L, SSD_L), 1)
    lane_lo = lax.broadcasted_iota(jnp.int32, (SSD_L, 128), 1) < SSD_P
    bias = bias_ref[...]
    a_all = -jnp.exp(alog_ref[...])
    dirs = ((xf_ref, bf_ref, cf_ref, dtf_ref, yf_ref), (xb_ref, bb_ref, cb_ref, dtb_ref, yb_ref))

    for d, (xs_ref, bm_ref, cm_ref, dt_ref, y_ref) in enumerate(dirs):
        tri = (row >= col) if d == 0 else (row <= col)
        tri_b = jnp.where(tri, 1.0, 0.0).astype(BF16)
        raw = dt_ref[...] + bias
        dt_all = jnp.maximum(raw, 0.0) + jnp.log1p(jnp.exp(-jnp.abs(raw)))
        acs = _dot3_l(tri_b, dt_all * a_all)
        acs_t = acs.T
        acs_hm = _hi_mid(acs)
        dt_e = _dot(_hi_mid(dt_all), ep_ref[d])
        acs_e = _dot(acs_hm, ep_ref[d])
        acs_b = _dot(acs_hm, ef_ref[d])
        last = SSD_L - 1 if d == 0 else 0
        tot_e = acs_e[last:last + 1, :]

        for g in range(SSD_G):
            cm_b = cm_ref[:, g * SSD_N:(g + 1) * SSD_N]
            bm = bm_ref[:, g * SSD_N:(g + 1) * SSD_N]
            cb = _dot_nt(cm_b, bm)
            bm_t = bm.astype(F32).T.astype(BF16)

            def head_mat(h):
                c = d * SSD_NH + h
                seg = acs_b[:, h * 128:(h + 1) * 128] - acs_t[c:c + 1, :]
                dec = jnp.exp(jnp.where(tri, seg, -jnp.inf))
                return (cb * dec).astype(BF16)

            for j in range(SSD_HPG // 2):
                h0 = g * SSD_HPG + 2 * j
                lanes = slice(g * GW + j * 128, g * GW + (j + 1) * 128)
                acs_p = acs_e[:, lanes]
                tot_p = tot_e[:, lanes]
                xdt = xs_ref[:, lanes].astype(F32) * dt_e[:, lanes]
                x_lo = jnp.where(lane_lo, xdt, 0.0).astype(BF16)
                x_hi = jnp.where(lane_lo, 0.0, xdt).astype(BF16)
                y_diag = _dot(head_mat(h0), x_lo) + _dot(head_mat(h0 + 1), x_hi)
                st_old = st_ref[d, g, j]
                y_off = _dot(cm_b, st_old.astype(BF16)) * jnp.exp(acs_p)
                to_end = jnp.exp(tot_p - acs_p)
                st_ref[d, g, j] = st_old * jnp.exp(tot_p) + _dot(bm_t, (xdt * to_end).astype(BF16))
                y_ref[:, lanes] = (y_diag + y_off).astype(BF16)


def _ssd(xbc_act, dt, bias_all, alog_all):
    nbc = SSD_INNER // (SSD_G * SSD_N)
    ep, ef = _ssd_expand_tables()
    fwd = lambda blk: (lambda b, i: (b, i, blk))
    bwd = lambda blk: (lambda b, i: (b, _ssd_bwd_chunk(i), blk))
    specs = lambda ix: [pl.BlockSpec((None, SSD_L, SSD_INNER), ix(0)),
                        pl.BlockSpec((None, SSD_L, SSD_G * SSD_N), ix(nbc)),
                        pl.BlockSpec((None, SSD_L, SSD_G * SSD_N), ix(nbc + 1)),
                        pl.BlockSpec((None, SSD_L, DT_PAD), ix(0))]
    par = pl.BlockSpec((1, DT_PAD), lambda b, i: (0, 0))
    full = lambda a: pl.BlockSpec(a.shape, lambda b, i: (0,) * a.ndim)
    return pl.pallas_call(
        _ssd_kernel,
        grid=(B, NCH_CAT),
        in_specs=specs(fwd) + specs(bwd) + [par, par, full(ep), full(ef)],
        out_specs=[pl.BlockSpec((None, SSD_L, SSD_INNER), fwd(0)),
                   pl.BlockSpec((None, SSD_L, SSD_INNER), bwd(0))],
        out_shape=[jax.ShapeDtypeStruct((B, TT, SSD_INNER), BF16)] * 2,
        scratch_shapes=[pltpu.VMEM((2, SSD_G, SSD_HPG // 2, SSD_N, 128), F32)],
        compiler_params=_cp(("parallel", "arbitrary")),
        name="ssd_scan",
    )(xbc_act, xbc_act, xbc_act, dt, xbc_act, xbc_act, xbc_act, dt, bias_all, alog_all, ep, ef)


def _dft_cs(n):
    k = np.arange(n, dtype=np.int64)
    ang = 2.0 * np.pi * ((k[:, None] * k[None, :]) % n).astype(np.float64) / n
    return np.cos(ang), np.sin(ang)


def _hi_lo(a):
    a32 = jnp.asarray(a, dtype=F32)
    hi = a32.astype(BF16)
    lo = (a32 - hi.astype(F32)).astype(BF16)
    return jnp.stack([hi, lo], axis=0)


def _fourier_tables():
    c1, s1 = _dft_cs(FT1)
    w1 = np.concatenate([c1, -s1], axis=0)
    r = np.arange(FT2)
    t2_of_r = F_OCT * (r % (FT2 // F_OCT)) + r // (FT2 // F_OCT)
    c2, s2 = _dft_cs(FT2)
    c2, s2 = c2[:, t2_of_r], s2[:, t2_of_r]
    w2 = np.block([[c2, s2], [-s2, c2]])
    cc, sc = _dft_cs(FGD)
    wc = np.concatenate([cc, sc], axis=0) / math.sqrt(T * FGD)
    ang = 2.0 * np.pi * t2_of_r.astype(np.float64)[:, None] / T
    tw = np.stack([np.cos(ang), np.sin(ang)], axis=0) * np.ones((1, 1, FGD))
    cx, sx = _dft_cs(TC)
    wx = np.concatenate([cx, sx], axis=0)
    wcx = np.concatenate([cc, -sc], axis=0) / math.sqrt(TC * FGD)
    return _hi_lo(w1), _hi_lo(w2), _hi_lo(wc), jnp.asarray(tw, dtype=F32), _hi_lo(wx), _hi_lo(wcx)


def _dot_hl(a_hl, x):
    xh = x.astype(BF16)
    xl = (x - xh.astype(F32)).astype(BF16)
    return _dot(a_hl[0], xh) + _dot(a_hl[0], xl) + _dot(a_hl[1], xh)


def _dot_lh(x, b_hl):
    xh = x.astype(BF16)
    xl = (x - xh.astype(F32)).astype(BF16)
    return _dot(xh, b_hl[0]) + _dot(xl, b_hl[0]) + _dot(xh, b_hl[1])


F_OCT = 8


def _fourier_kernel(x_ref, w1_ref, w2_ref, wc_ref, tw_ref, wx_ref, wcx_ref, o_ref, y1_ref):
    w1 = (w1_ref[0], w1_ref[1])
    w2 = (w2_ref[0], w2_ref[1])
    wc = (wc_ref[0], wc_ref[1])

    noct = FT2 // F_OCT
    for j in range(noct):
        cols = [x_ref[pl.ds(TC + F_OCT * j + q, FT1, stride=FT2), :] for q in range(F_OCT)]
        y = _dot_hl(w1, jnp.concatenate(cols, axis=1))
        for q in range(F_OCT):
            y1_ref[q, j * 2 * FT1:(j + 1) * 2 * FT1, :] = y[:, q * FGD:(q + 1) * FGD]

    def gather(kk):
        return jnp.concatenate([y1_ref[q, pl.ds(kk, noct, stride=2 * FT1), :] for q in range(F_OCT)], axis=0)

    cstep = tw_ref[0]
    sstep = tw_ref[1]

    def body(k1, carry):
        c, s = carry
        yr = gather(k1)
        yi = gather(FT1 + k1)
        y2 = jnp.concatenate([yr * c + yi * s, yi * c - yr * s], axis=0)
        y3 = _dot_hl(w2, y2)
        lhs = jnp.concatenate([y3[:FT2, :], y3[FT2:, :]], axis=1)
        o_ref[pl.ds(TC + k1, FT2, stride=FT1), :] = _dot_lh(lhs, wc)
        return c * cstep - s * sstep, s * cstep + c * sstep

    init = (jnp.ones((FT2, FGD), F32), jnp.zeros((FT2, FGD), F32))
    lax.fori_loop(0, FT1, body, init, unroll=4)

    ab = _dot_hl((wx_ref[0], wx_ref[1]), x_ref[0:TC, :])
    lhs = jnp.concatenate([ab[:TC, :], ab[TC:, :]], axis=1)
    o_ref[0:TC, :] = _dot_lh(lhs, (wcx_ref[0], wcx_ref[1]))


def _fourier(f_cat):
    w1, w2, wc, tw, wx, wcx = _fourier_tables()
    full = lambda a: pl.BlockSpec(a.shape, lambda b, g: (0,) * a.ndim)
    blk = pl.BlockSpec((None, TT, FGD), lambda b, g: (b, 0, g))
    return pl.pallas_call(
        _fourier_kernel,
        grid=(B, FG),
        in_specs=[blk, full(w1), full(w2), full(wc), full(tw), full(wx), full(wcx)],
        out_specs=blk,
        out_shape=jax.ShapeDtypeStruct((B, TT, FW), F32),
        scratch_shapes=[pltpu.VMEM((F_OCT, FT2 // F_OCT * 2 * FT1, FGD), F32)],
        compiler_params=_cp(("parallel", "parallel")),
        name="fourier_mix",
    )(f_cat, w1, w2, wc, tw, wx, wcx)


def _even_out_kernel(s_ref, fo_ref, yf_ref, yb_ref, xs_ref, z_ref, gn_ref, ds_ref, wf_ref, wy_ref, g_ref,
                     mod_ref, o_ref):
    y = yf_ref[...].astype(F32) + yb_ref[...].astype(F32) + ds_ref[...] * xs_ref[...].astype(F32)
    gated = y * _silu(z_ref[...].astype(F32))
    gn = gn_ref[...]
    parts = []
    for g in range(SSD_G):
        parts.append(_rms(gated[:, g * GW:(g + 1) * GW], gn[:, g * GW:(g + 1) * GW]))
    yn = jnp.concatenate(parts, axis=1).astype(BF16)
    o = _dot(fo_ref[...].astype(BF16), wf_ref[...]) + _dot(yn, wy_ref[...])
    o_ref[...] = s_ref[...] + mod_ref[2:3, :] * _rms(o, g_ref[...])


def _even_out(s, fo, yf, yb, xbc_act, z, gn, dsum, wf, wy, g, modtab):
    row = lambda n: pl.BlockSpec((None, TM, n), lambda b, i: (b, i, 0))
    full = lambda a: pl.BlockSpec(a.shape, lambda b, i: (0,) * a.ndim)
    return pl.pallas_call(
        _even_out_kernel,
        grid=(B, NT_CAT),
        in_specs=[row(D), row(FW), row(SSD_INNER), row(SSD_INNER),
                  row(SSD_INNER), row(SSD_INNER), full(gn), full(dsum), full(wf), full(wy), full(g),
                  pl.BlockSpec((None, None, 8, D), _cat_mod_idx)],
        out_specs=row(D),
        out_shape=jax.ShapeDtypeStruct((B, TT, D), F32),
        compiler_params=_cp(("parallel", "parallel"), 56),
        name="even_out_proj",
    )(s, fo, yf, yb, xbc_act, z, gn, dsum, wf, wy, g, modtab)


def _ffn_kernel(s_ref, g2_ref, g3_ref, mod_ref, wg_ref, wu_ref, wd_ref, o_ref):
    a = _norm_mod_in(s_ref, g2_ref, mod_ref, 3).astype(BF16)
    gt = _dot(a, wg_ref[...])
    up = _dot(a, wu_ref[...])
    hid = (_silu(gt) * up).astype(BF16)
    f = _dot(hid, wd_ref[...])
    o_ref[...] = s_ref[...] + mod_ref[5:6, :] * _rms(f, g3_ref[...])


def _ffn(s, g2, g3, modtab, wg, wu, wd, cat):
    nt = NT_CAT if cat else NT_LAT
    rows = TT if cat else T
    row = pl.BlockSpec((None, TM, D), lambda b, i: (b, i, 0))
    full = lambda a: pl.BlockSpec(a.shape, lambda b, i: (0,) * a.ndim)
    return pl.pallas_call(
        _ffn_kernel,
        grid=(B, nt),
        in_specs=[row, full(g2), full(g3),
                  pl.BlockSpec((None, None, 8, D), _cat_mod_idx if cat else _lat_mod_idx),
                  full(wg), full(wu), full(wd)],
        out_specs=row,
        out_shape=jax.ShapeDtypeStruct((B, rows, D), F32),
        compiler_params=_cp(("parallel", "parallel"), 60),
        name="ffn_cat" if cat else "ffn_lat",
    )(s, g2, g3, modtab, wg, wu, wd)


def _rope_tables():
    rows = T // GRID_W
    row = jnp.repeat(jnp.arange(rows, dtype=F32), GRID_W)
    col = jnp.tile(jnp.arange(GRID_W, dtype=F32), rows)
    nf = QK // 4
    inv_freq = ROPE_BASE ** (-jnp.arange(nf, dtype=F32) * 2.0 / (QK // 2))
    ang = jnp.stack([row, col], axis=-1)[:, :, None] * inv_freq
    cos, sin = jnp.cos(ang), jnp.sin(ang)
    lane = np.arange(128)
    axis = (lane % QK) // (QK // 2)
    half = (lane % (QK // 2)) // nf
    freq = lane % nf
    cos_l = cos[:, axis, freq]
    sin_l = sin[:, axis, freq]
    sa = jnp.where(jnp.asarray(half == 0)[None, :], -sin_l, 0.0)
    sb = jnp.where(jnp.asarray(half == 1)[None, :], sin_l, 0.0)
    ctx1 = jnp.ones((TC, 128), F32)
    ctx0 = jnp.zeros((TC, 128), F32)
    return (jnp.concatenate([ctx1, cos_l], axis=0), jnp.concatenate([ctx0, sa], axis=0),
            jnp.concatenate([ctx0, sb], axis=0))


def _odd_in_kernel(s_ref, g_ref, mod_ref, wq_ref, wk_ref, wv_ref, wu_ref, cos_ref, sa_ref, sb_ref,
                   q_ref, k_ref, v_ref, u_ref):
    a = _norm_mod_in(s_ref, g_ref, mod_ref, 0).astype(BF16)
    cos = cos_ref[...]
    sa = sa_ref[...]
    sb = sb_ref[...]

    def rope(x, scale):
        outs = []
        for h in range(NH):
            xh = x[:, h * 128:(h + 1) * 128]
            r = xh * cos + pltpu.roll(xh, 112, 1) * sa + pltpu.roll(xh, 16, 1) * sb
            outs.append(r * scale if scale != 1.0 else r)
        return jnp.concatenate(outs, axis=1)

    q_ref[...] = rope(_dot(a, wq_ref[...]), ATTN_SCALE * math.log2(math.e)).astype(BF16)
    k_ref[...] = rope(_dot(a, wk_ref[...]), 1.0).astype(BF16)
    v_ref[...] = _dot(a, wv_ref[...]).astype(BF16)
    u_ref[...] = _dot(a, wu_ref[...]).astype(BF16)


def _odd_in(s, g, modtab, wq, wk, wv, wu, cos, sa, sb):
    row = lambda n: pl.BlockSpec((None, TM, n), lambda b, i: (b, i, 0))
    full = lambda a: pl.BlockSpec(a.shape, lambda b, i: (0,) * a.ndim)
    tab = pl.BlockSpec((TM, 128), lambda b, i: (i, 0))
    return pl.pallas_call(
        _odd_in_kernel,
        grid=(B, NT_CAT),
        in_specs=[row(D), full(g), pl.BlockSpec((None, None, 8, D), _cat_mod_idx),
                  full(wq), full(wk), full(wv), full(wu), tab, tab, tab],
        out_specs=[row(D), row(D), row(D), row(D)],
        out_shape=[jax.ShapeDtypeStruct((B, TT, D), BF16)] * 4,
        compiler_params=_cp(("parallel", "parallel"), 56),
        name="odd_in_proj",
    )(s, g, modtab, wq, wk, wv, wu, cos, sa, sb)


def _attn_kernel(qa_ref, qb_ref, qn_ref, k_ref, v_ref, lam_ref, sg_ref, o_ref,
                 q2_ref, s_ref, mp_ref, mb_ref, ls_ref, acc_ref, *, lambda_init):
    i = pl.program_id(2)
    lane_lo = lax.broadcasted_iota(jnp.int32, (TM, 128), 1) < QK

    def stack(src_ref, slot):
        q = src_ref[...]
        zero = jnp.zeros_like(q)
        q2_ref[slot, 0:TM, :] = jnp.where(lane_lo, q, zero)
        q2_ref[slot, TM:2 * TM, :] = jnp.where(lane_lo, zero, q)

    stack(qa_ref, 0)
    stack(qb_ref, 1)
    stack(qn_ref, 2)

    def produce(c, qslot, sslot):
        r0 = pl.multiple_of(c * KT, KT)
        s = _dot_nt(q2_ref[qslot], k_ref[pl.ds(r0, KT), :])
        s_ref[sslot, c] = s
        mp_ref[...] = jnp.maximum(mp_ref[...], jnp.maximum(s[:, 0:128], s[:, 128:256]))

    def finish_max():
        m = jnp.max(mp_ref[...], axis=-1, keepdims=True)
        mb_ref[...] = jnp.broadcast_to(m, mb_ref.shape)
        mp_ref[...] = jnp.full(mp_ref.shape, -jnp.inf, F32)

    def consume(c, sslot):
        r0 = pl.multiple_of(c * KT, KT)
        s = s_ref[sslot, c]
        mb = mb_ref[...]
        p0 = jnp.exp2(s[:, 0:128] - mb)
        p1 = jnp.exp2(s[:, 128:256] - mb)
        ls_ref[...] += p0 + p1
        p = jnp.concatenate([p0, p1], axis=1).astype(BF16)
        acc_ref[...] += _dot(p, v_ref[pl.ds(r0, KT), :])

    lf = lam_ref[...]
    lam = (jnp.exp(jnp.sum(lf[0:1, :] * lf[1:2, :], axis=-1, keepdims=True))
           - jnp.exp(jnp.sum(lf[2:3, :] * lf[3:4, :], axis=-1, keepdims=True)) + lambda_init)

    def phase(cur_slot, next_q, next_slot, out_rows):
        ls_ref[...] = jnp.zeros(ls_ref.shape, F32)
        acc_ref[...] = jnp.zeros(acc_ref.shape, F32)

        def body(c, carry):
            produce(c, next_q, next_slot)
            consume(c, cur_slot)
            return carry

        lax.fori_loop(0, NKT, body, 0, unroll=11)
        finish_max()
        o = acc_ref[...] / jnp.sum(ls_ref[...], axis=-1, keepdims=True)
        od = o[0:TM, :] - lam * o[TM:2 * TM, :]
        o_ref[out_rows, :] = (_rms(od, sg_ref[...]) * (1.0 - lambda_init)).astype(BF16)

    @pl.when(i == 0)
    def _():
        mp_ref[...] = jnp.full(mp_ref.shape, -jnp.inf, F32)

        def body(c, carry):
            produce(c, 0, 0)
            return carry

        lax.fori_loop(0, NKT, body, 0)
        finish_max()

    phase(0, 1, 1, slice(0, TM))
    phase(1, 2, 0, slice(TM, 2 * TM))


def _attention(q, k, v, lam, subln_g, lambda_init):
    nsteps = NT_LAT // 2
    return pl.pallas_call(
        functools.partial(_attn_kernel, lambda_init=lambda_init),
        grid=(B, NH, nsteps),
        in_specs=[pl.BlockSpec((None, TM, 128), lambda b, h, i: (b, 2 * i + 1, h)),
                  pl.BlockSpec((None, TM, 128), lambda b, h, i: (b, 2 * i + 2, h)),
                  pl.BlockSpec((None, TM, 128), lambda b, h, i: (b, jnp.minimum(2 * i + 3, NT_LAT), h)),
                  pl.BlockSpec((None, TT, 128), lambda b, h, i: (b, 0, h)),
                  pl.BlockSpec((None, TT, 128), lambda b, h, i: (b, 0, h)),
                  pl.BlockSpec((4, QK), lambda b, h, i: (0, 0)),
                  pl.BlockSpec((1, DV), lambda b, h, i: (0, 0))],
        out_specs=pl.BlockSpec((None, 2 * TM, 128), lambda b, h, i: (b, i, h)),
        out_shape=jax.ShapeDtypeStruct((B, T, NH * DV), BF16),
        scratch_shapes=[pltpu.VMEM((3, 2 * TM, 128), BF16),
                        pltpu.VMEM((2, NKT, 2 * TM, KT), F32),
                        pltpu.VMEM((2 * TM, 128), F32),
                        pltpu.VMEM((2 * TM, 128), F32),
                        pltpu.VMEM((2 * TM, 128), F32),
                        pltpu.VMEM((2 * TM, DV), F32)],
        compiler_params=_cp(("parallel", "parallel", "arbitrary"), 58),
        name="diff_attention",
    )(q, q, q, k, v, lam, subln_g)


CONF_HALO = 16


def _conf_conv_kernel(a_ref, gt_ref, w_ref, b_ref, o_ref, vs_ref):
    def fill(c, carry):
        r0 = pl.multiple_of(c * TM, TM)
        v = a_ref[pl.ds(r0, TM), :].astype(F32) * _sigmoid(gt_ref[pl.ds(r0, TM), :].astype(F32))
        vs_ref[pl.ds(r0, TM), :] = jnp.where(c >= 1, v, 0.0)
        return carry

    lax.fori_loop(0, NT_CAT, fill, 0)
    vs_ref[TT:TT + CONF_HALO, :] = jnp.zeros((CONF_HALO, 128), F32)
    w = w_ref[...]
    bias = b_ref[...]

    def body(c, carry):
        start = pl.multiple_of(c * TM + (TC - CONF_HALO), 8)
        win = vs_ref[pl.ds(start, TM + 2 * CONF_HALO), :]
        acc = jnp.broadcast_to(bias, (TM, 128))
        for k in range(CONF_K):
            off = CONF_HALO - CONF_K // 2 + k
            acc = acc + w[k:k + 1, :] * win[off:off + TM, :]
        o_ref[pl.ds(pl.multiple_of(c * TM, TM), TM), :] = acc
        return carry

    lax.fori_loop(0, NT_LAT, body, 0)


def _conf_conv(u, w32, bias):
    nct = CONF_C // 128
    return pl.pallas_call(
        _conf_conv_kernel,
        grid=(B, nct),
        in_specs=[pl.BlockSpec((None, TT, 128), lambda b, c: (b, 0, c)),
                  pl.BlockSpec((None, TT, 128), lambda b, c: (b, 0, nct + c)),
                  pl.BlockSpec((32, 128), lambda b, c: (0, c)),
                  pl.BlockSpec((1, 128), lambda b, c: (0, c))],
        out_specs=pl.BlockSpec((None, T, 128), lambda b, c: (b, 0, c)),
        out_shape=jax.ShapeDtypeStruct((B, T, CONF_C), F32),
        scratch_shapes=[pltpu.VMEM((TT + CONF_HALO, 128), F32)],
        compiler_params=_cp(("parallel", "parallel")),
        name="conformer_conv31",
    )(u, u, w32, bias)


def _odd_out_kernel(s_ref, oa_ref, cv_ref, cg_ref, cb_ref, wa_ref, wc_ref, g_ref, mod_ref, o_ref):
    v = cv_ref[...]
    mu = jnp.mean(v, axis=-1, keepdims=True)
    vc = v - mu
    var = jnp.mean(vc * vc, axis=-1, keepdims=True)
    cv = _silu(vc * lax.rsqrt(var + EPS) * cg_ref[...] + cb_ref[...])
    o = _dot(oa_ref[...], wa_ref[...]) + _dot(cv.astype(BF16), wc_ref[...])
    o_ref[...] = s_ref[...] + mod_ref[2:3, :] * _rms(o, g_ref[...])


def _odd_out(s_cat, oa, cvpre, cg, cb, wa, wc, g, modtab):
    full = lambda a: pl.BlockSpec(a.shape, lambda b, i: (0,) * a.ndim)
    row = lambda n: pl.BlockSpec((None, TM, n), lambda b, i: (b, i, 0))
    return pl.pallas_call(
        _odd_out_kernel,
        grid=(B, NT_LAT),
        in_specs=[pl.BlockSpec((None, TM, D), lambda b, i: (b, i + 1, 0)), row(NH * DV), row(CONF_C),
                  full(cg), full(cb), full(wa), full(wc), full(g),
                  pl.BlockSpec((None, None, 8, D), _lat_mod_idx)],
        out_specs=row(D),
        out_shape=jax.ShapeDtypeStruct((B, T, D), F32),
        compiler_params=_cp(("parallel", "parallel")),
        name="odd_out_proj",
    )(s_cat, oa, cvpre, cg, cb, wa, wc, g, modtab)


def _pad_rows(a, n):
    return jnp.pad(a, ((0, n - a.shape[0]), (0, 0)))


def _pad_lanes(a, n):
    return jnp.pad(a, ((0, 0), (0, n - a.shape[1])))


def kernel(x, c, ctx, c_ctx, mod_w, mod_b, norm_g, ffn_w_gate, ffn_w_up, ffn_w_down, ev_w_in, ev_conv_w, ev_conv_b, ev_dt_bias, ev_a_log, ev_d_skip, ev_gnorm_g, ev_w_out, od_w_in, od_lambda, od_subln_g, od_conv_w, od_conv_b, od_cnorm_g, od_cnorm_b, od_w_out):
    bf = lambda a: a.astype(BF16)

    cvec = _pad_rows(jnp.concatenate([c, c_ctx[None, :]], axis=0), 8)
    mv = _mod_vectors(cvec, mod_w, mod_b).reshape(DEPTH, 8, N_MOD, D)
    modtabs = []
    for l in range(DEPTH):
        lat = mv[l, :B]
        cx = jnp.broadcast_to(mv[l, B][None], (B, N_MOD, D))
        tab = jnp.stack([cx, lat], axis=1)
        modtabs.append(jnp.pad(tab, ((0, 0), (0, 0), (0, 8 - N_MOD), (0, 0))))

    s = jnp.concatenate([ctx, x], axis=1)

    g = norm_g[0]
    w_in = ev_w_in[0]
    o0, o1, o2 = FW, FW + SSD_INNER, FW + SSD_INNER + SSD_XBC
    f, z, xbc, dt = _even_in(s, g[0:1], modtabs[0], bf(w_in[:, :o0]), bf(w_in[:, o0:o1]),
                             bf(w_in[:, o1:o2]), bf(_pad_lanes(w_in[:, o2:], DT_PAD)))
    xbc_act = _conv5(xbc, _pad_rows(ev_conv_w[0], 8), ev_conv_b[0][None, :])
    bias_all = _pad_lanes(ev_dt_bias[0].reshape(1, -1), DT_PAD)
    alog_all = _pad_lanes(ev_a_log[0].reshape(1, -1), DT_PAD)
    yf, yb = _ssd(xbc_act, dt, bias_all, alog_all)
    fo = _fourier(f)
    dsum = jnp.repeat(ev_d_skip[0, 0] + ev_d_skip[0, 1], SSD_P)[None, :]
    w_out = ev_w_out[0]
    s = _even_out(s, fo, yf, yb, xbc_act, z, ev_gnorm_g[0][None, :], dsum, bf(w_out[:FW]), bf(w_out[FW:]),
                  g[1:2], modtabs[0])
    s = _ffn(s, g[2:3], g[3:4], modtabs[0], bf(ffn_w_gate[0]), bf(ffn_w_up[0]), bf(ffn_w_down[0]), True)

    g = norm_g[1]
    lambda_init = 0.8 - 0.6 * math.exp(-0.3 * 1)
    w_in = od_w_in[0]
    cos, sa, sb = _rope_tables()
    q, k, v, u = _odd_in(s, g[0:1], modtabs[1], bf(w_in[:, :D]), bf(w_in[:, D:2 * D]),
                         bf(w_in[:, 2 * D:3 * D]), bf(w_in[:, 3 * D:]), cos, sa, sb)
    oa = _attention(q, k, v, od_lambda[0], od_subln_g[0][None, :], lambda_init)
    cvpre = _conf_conv(u, _pad_rows(od_conv_w[0], 32), od_conv_b[0][None, :])
    w_out = od_w_out[0]
    h = _odd_out(s, oa, cvpre, od_cnorm_g[0][None, :], od_cnorm_b[0][None, :], bf(w_out[:NH * DV]),
                 bf(w_out[NH * DV:]), g[1:2], modtabs[1])
    return _ffn(h, g[2:3], g[3:4], modtabs[1], bf(ffn_w_gate[1]), bf(ffn_w_up[1]), bf(ffn_w_down[1]), False)
```

```python
import functools
import math

import numpy as np
import jax
import jax.numpy as jnp
from jax import lax
from jax.experimental import pallas as pl
from jax.experimental.pallas import tpu as pltpu

F32 = jnp.float32
BF16 = jnp.bfloat16
HIGHEST = lax.Precision.HIGHEST

D = 1024
B = 2
T = 8192
TC = 256
TT = TC + T
DEPTH = 2
N_MOD = 6
HID = 2816
EPS = 1e-6
GRID_W = 64

TM = 256
NT_CAT = TT // TM
NT_LAT = T // TM

FW = 512
FG = 4
FGD = 128
SSD_INNER = 1536
SSD_G = 4
SSD_HPG = 6
SSD_P = 64
SSD_N = 128
SSD_L = 128
SSD_XBC = 2560
SSD_K = 5
NCH_CAT = TT // SSD_L
NCH_CTX = TC // SSD_L
GW = SSD_HPG * SSD_P
DT_PAD = 128

NH = 8
QK = 64
DV = 128
ATTN_SCALE = QK ** -0.5
CONF_C = 512
CONF_K = 31
ROPE_BASE = 10000.0
TQ = 512
KT = 256
NKT = TT // KT

FT1 = 64
FT2 = 128


def _cp(sem, vmem_mb=48):
    return pltpu.CompilerParams(dimension_semantics=sem, vmem_limit_bytes=vmem_mb * 1024 * 1024)


def _sigmoid(x):
    return 1.0 / (1.0 + jnp.exp(-x))


def _silu(x):
    return x * _sigmoid(x)


def _rms(x, g):
    ms = jnp.mean(x * x, axis=-1, keepdims=True)
    return x * lax.rsqrt(ms + EPS) * g


def _dot(a, b):
    return jnp.dot(a, b, preferred_element_type=F32)


def _dot_nt(a, b):
    return lax.dot_general(a, b, (((1,), (1,)), ((), ())), preferred_element_type=F32)


def _split3(x):
    hi = x.astype(BF16)
    r1 = x - hi.astype(F32)
    mid = r1.astype(BF16)
    lo = (r1 - mid.astype(F32)).astype(BF16)
    return hi, mid, lo


def _dot3_l(m01, x):
    hi, mid, lo = _split3(x)
    return _dot(m01, hi) + _dot(m01, mid) + _dot(m01, lo)


def _dot3_r(x, m01):
    hi, mid, lo = _split3(x)
    return _dot(hi, m01) + _dot(mid, m01) + _dot(lo, m01)


def _mod_kernel(c_ref, w_ref, b_ref, o_ref):
    a = _silu(c_ref[...])
    o_ref[...] = jnp.dot(a, w_ref[...], precision=HIGHEST, preferred_element_type=F32) + b_ref[...]


def _mod_vectors(cvec, mod_w, mod_b):
    tn = 1536
    return pl.pallas_call(
        _mod_kernel,
        grid=(DEPTH, N_MOD * D // tn),
        in_specs=[
            pl.BlockSpec((8, D), lambda l, j: (0, 0)),
            pl.BlockSpec((None, D, tn), lambda l, j: (l, 0, j)),
            pl.BlockSpec((None, 1, tn), lambda l, j: (l, 0, j)),
        ],
        out_specs=pl.BlockSpec((None, 8, tn), lambda l, j: (l, 0, j)),
        out_shape=jax.ShapeDtypeStruct((DEPTH, 8, N_MOD * D), F32),
        compiler_params=_cp(("parallel", "parallel")),
        name="mod_vectors",
    )(cvec, mod_w, mod_b.reshape(DEPTH, 1, N_MOD * D))


def _cat_mod_idx(b, i):
    return (b, jnp.minimum(i, 1), 0, 0)


def _lat_mod_idx(b, i):
    return (b, 1, 0, 0)


def _norm_mod_in(x, g_ref, mod_ref, row):
    y = _rms(x, g_ref[...])
    return y * (1.0 + mod_ref[row + 1:row + 2, :]) + mod_ref[row:row + 1, :]


def _stream_tile(ctx_ref, lat_ref):
    return jnp.where(pl.program_id(1) == 0, ctx_ref[...], lat_ref[...])


def _stream_specs():
    return [pl.BlockSpec((None, TM, D), lambda b, i: (b, 0, 0)),
            pl.BlockSpec((None, TM, D), lambda b, i: (b, jnp.maximum(i - 1, 0), 0))]


def _even_in_kernel(c_ref, l_ref, g_ref, mod_ref, wf_ref, wz_ref, wx_ref, wd_ref, f_ref, z_ref, x_ref, dt_ref):
    a = _norm_mod_in(_stream_tile(c_ref, l_ref), g_ref, mod_ref, 0).astype(BF16)
    f_ref[...] = _dot(a, wf_ref[...])
    z_ref[...] = _dot(a, wz_ref[...]).astype(BF16)
    x_ref[...] = _dot(a, wx_ref[...]).astype(BF16)
    dt_ref[...] = _dot(a, wd_ref[...])


def _even_in(ctx, x, g, modtab, wf, wz, wx, wd):
    row = lambda n: pl.BlockSpec((None, TM, n), lambda b, i: (b, i, 0))
    full = lambda a: pl.BlockSpec(a.shape, lambda b, i: (0,) * a.ndim)
    return pl.pallas_call(
        _even_in_kernel,
        grid=(B, NT_CAT),
        in_specs=_stream_specs() + [full(g), pl.BlockSpec((None, None, 8, D), _cat_mod_idx),
                                    full(wf), full(wz), full(wx), full(wd)],
        out_specs=[row(FW), row(SSD_INNER), row(SSD_XBC), row(DT_PAD)],
        out_shape=[jax.ShapeDtypeStruct((B, TT, n), dt) for n, dt in
                   ((FW, F32), (SSD_INNER, BF16), (SSD_XBC, BF16), (DT_PAD, F32))],
        compiler_params=_cp(("parallel", "parallel"), 56),
        name="even_in_proj",
    )(ctx, x, g, modtab, wf, wz, wx, wd)


CONV_HALO = 16


def _conv5_kernel(x_ref, w_ref, b_ref, o_ref):
    w = w_ref[...]
    bias = b_ref[...]

    def body(c, carry):
        r0 = pl.multiple_of(c * TM, TM)
        center = x_ref[pl.ds(r0, TM), :].astype(F32)
        lo = pl.multiple_of(jnp.maximum(r0 - CONV_HALO, 0), CONV_HALO)
        hi = pl.multiple_of(jnp.minimum(r0 + TM, TT - CONV_HALO), CONV_HALO)
        left = jnp.where(c >= 2, x_ref[pl.ds(lo, CONV_HALO), :].astype(F32), 0.0)
        right = jnp.where(jnp.logical_and(c >= 1, c < NT_CAT - 1),
                          x_ref[pl.ds(hi, CONV_HALO), :].astype(F32), 0.0)
        ext = jnp.concatenate([left, center, right], axis=0)
        acc = bias + w[2:3, :] * center
        for k in (0, 1, 3, 4):
            off = CONV_HALO - SSD_K // 2 + k
            acc = acc + w[k:k + 1, :] * ext[off:off + TM, :]
        o_ref[pl.ds(r0, TM), :] = _silu(acc).astype(BF16)
        return carry

    lax.fori_loop(0, NT_CAT, body, 0)


def _conv5(xbc, w8, bias):
    return pl.pallas_call(
        _conv5_kernel,
        grid=(B, SSD_XBC // 128),
        in_specs=[
            pl.BlockSpec((None, TT, 128), lambda b, c: (b, 0, c)),
            pl.BlockSpec((8, 128), lambda b, c: (0, c)),
            pl.BlockSpec((1, 128), lambda b, c: (0, c)),
        ],
        out_specs=pl.BlockSpec((None, TT, 128), lambda b, c: (b, 0, c)),
        out_shape=jax.ShapeDtypeStruct((B, TT, SSD_XBC), BF16),
        compiler_params=_cp(("parallel", "parallel")),
        name="ssd_conv5",
    )(xbc, w8, bias)


def _ssd_bwd_chunk(i):
    return jnp.where(i < NCH_CTX, NCH_CTX - 1 - i, NCH_CAT + NCH_CTX - 1 - i)


SSD_NH = SSD_G * SSD_HPG


def _ssd_expand_tables():
    lane = np.arange(DT_PAD)[:, None]
    col = np.arange(SSD_INNER)[None, :]
    colf = np.arange(SSD_NH * 128)[None, :]
    pair, full = [], []
    for d in range(2):
        head_of_col = d * SSD_NH + 2 * (col // 128) + (col % 128) // SSD_P
        e = (lane == head_of_col).astype(np.float32)
        pair.append(np.concatenate([e, e], axis=0))
        ef = (lane == d * SSD_NH + colf // 128).astype(np.float32)
        full.append(np.concatenate([ef, ef], axis=0))
    return jnp.asarray(np.stack(pair), dtype=BF16), jnp.asarray(np.stack(full), dtype=BF16)


def _hi_mid(x):
    hi = x.astype(BF16)
    mid = (x - hi.astype(F32)).astype(BF16)
    return jnp.concatenate([hi, mid], axis=1)


def _ssd_kernel(xf_ref, bf_ref, cf_ref, dtf_ref, xb_ref, bb_ref, cb_ref, dtb_ref, bias_ref, alog_ref,
                ep_ref, ef_ref, yf_ref, yb_ref, st_ref):
    @pl.when(pl.program_id(1) == 0)
    def _():
        st_ref[...] = jnp.zeros_like(st_ref)

    row = lax.broadcasted_iota(jnp.int32, (SSD_L, SSD_L), 0)
    col = lax.broadcasted_iota(jnp.int32, (SSD_L, SSD_L), 1)
    lane_lo = lax.broadcasted_iota(jnp.int32, (SSD_L, 128), 1) < SSD_P
    bias = bias_ref[...]
    a_all = -jnp.exp(alog_ref[...])
    dirs = ((xf_ref, bf_ref, cf_ref, dtf_ref, yf_ref), (xb_ref, bb_ref, cb_ref, dtb_ref, yb_ref))

    for d, (xs_ref, bm_ref, cm_ref, dt_ref, y_ref) in enumerate(dirs):
        tri = (row >= col) if d == 0 else (row <= col)
        tri_b = jnp.where(tri, 1.0, 0.0).astype(BF16)
        raw = dt_ref[...] + bias
        dt_all = jnp.maximum(raw, 0.0) + jnp.log1p(jnp.exp(-jnp.abs(raw)))
        acs = _dot3_l(tri_b, dt_all * a_all)
        acs_t = acs.T
        acs_hm = _hi_mid(acs)
        dt_e = _dot(_hi_mid(dt_all), ep_ref[d])
        acs_e = _dot(acs_hm, ep_ref[d])
        acs_b = _dot(acs_hm, ef_ref[d])
        last = SSD_L - 1 if d == 0 else 0
        tot_e = acs_e[last:last + 1, :]

        for g in range(SSD_G):
            cm_b = cm_ref[:, g * SSD_N:(g + 1) * SSD_N]
            bm = bm_ref[:, g * SSD_N:(g + 1) * SSD_N]
            cb = _dot_nt(cm_b, bm)
            bm_t = bm.astype(F32).T.astype(BF16)

            def head_mat(h):
                c = d * SSD_NH + h
                seg = acs_b[:, h * 128:(h + 1) * 128] - acs_t[c:c + 1, :]
                dec = jnp.exp(jnp.where(tri, seg, -jnp.inf))
                return (cb * dec).astype(BF16)

            for j in range(SSD_HPG // 2):
                h0 = g * SSD_HPG + 2 * j
                lanes = slice(g * GW + j * 128, g * GW + (j + 1) * 128)
                acs_p = acs_e[:, lanes]
                tot_p = tot_e[:, lanes]
                xdt = xs_ref[:, lanes].astype(F32) * dt_e[:, lanes]
                x_lo = jnp.where(lane_lo, xdt, 0.0).astype(BF16)
                x_hi = jnp.where(lane_lo, 0.0, xdt).astype(BF16)
                y_diag = _dot(head_mat(h0), x_lo) + _dot(head_mat(h0 + 1), x_hi)
                st_old = st_ref[d, g, j]
                y_off = _dot(cm_b, st_old.astype(BF16)) * jnp.exp(acs_p)
                to_end = jnp.exp(tot_p - acs_p)
                st_ref[d, g, j] = st_old * jnp.exp(tot_p) + _dot(bm_t, (xdt * to_end).astype(BF16))
                y_ref[:, lanes] = (y_diag + y_off).astype(BF16)


def _ssd(xbc_act, dt, bias_all, alog_all):
    nbc = SSD_INNER // (SSD_G * SSD_N)
    ep, ef = _ssd_expand_tables()
    fwd = lambda blk: (lambda b, i: (b, i, blk))
    bwd = lambda blk: (lambda b, i: (b, _ssd_bwd_chunk(i), blk))
    specs = lambda ix: [pl.BlockSpec((None, SSD_L, SSD_INNER), ix(0)),
                        pl.BlockSpec((None, SSD_L, SSD_G * SSD_N), ix(nbc)),
                        pl.BlockSpec((None, SSD_L, SSD_G * SSD_N), ix(nbc + 1)),
                        pl.BlockSpec((None, SSD_L, DT_PAD), ix(0))]
    par = pl.BlockSpec((1, DT_PAD), lambda b, i: (0, 0))
    full = lambda a: pl.BlockSpec(a.shape, lambda b, i: (0,) * a.ndim)
    return pl.pallas_call(
        _ssd_kernel,
        grid=(B, NCH_CAT),
        in_specs=specs(fwd) + specs(bwd) + [par, par, full(ep), full(ef)],
        out_specs=[pl.BlockSpec((None, SSD_L, SSD_INNER), fwd(0)),
                   pl.BlockSpec((None, SSD_L, SSD_INNER), bwd(0))],
        out_shape=[jax.ShapeDtypeStruct((B, TT, SSD_INNER), BF16)] * 2,
        scratch_shapes=[pltpu.VMEM((2, SSD_G, SSD_HPG // 2, SSD_N, 128), F32)],
        compiler_params=_cp(("parallel", "arbitrary")),
        name="ssd_scan",
    )(xbc_act, xbc_act, xbc_act, dt, xbc_act, xbc_act, xbc_act, dt, bias_all, alog_all, ep, ef)


def _dft_cs(n):
    k = np.arange(n, dtype=np.int64)
    ang = 2.0 * np.pi * ((k[:, None] * k[None, :]) % n).astype(np.float64) / n
    return np.cos(ang), np.sin(ang)


def _hi_lo(a):
    a32 = jnp.asarray(a, dtype=F32)
    hi = a32.astype(BF16)
    lo = (a32 - hi.astype(F32)).astype(BF16)
    return jnp.stack([hi, lo], axis=0)


def _fourier_tables():
    c1, s1 = _dft_cs(FT1)
    w1 = np.concatenate([c1, -s1], axis=0)
    r = np.arange(FT2)
    t2_of_r = F_OCT * (r % (FT2 // F_OCT)) + r // (FT2 // F_OCT)
    c2, s2 = _dft_cs(FT2)
    c2, s2 = c2[:, t2_of_r], s2[:, t2_of_r]
    w2 = np.block([[c2, s2], [-s2, c2]])
    cc, sc = _dft_cs(FGD)
    wc = np.concatenate([cc, sc], axis=0) / math.sqrt(T * FGD)
    ang = 2.0 * np.pi * t2_of_r.astype(np.float64)[:, None] / T
    tw = np.stack([np.cos(ang), np.sin(ang)], axis=0) * np.ones((1, 1, FGD))
    cx, sx = _dft_cs(TC)
    wx = np.concatenate([cx, sx], axis=0)
    wcx = np.concatenate([cc, -sc], axis=0) / math.sqrt(TC * FGD)
    return _hi_lo(w1), _hi_lo(w2), _hi_lo(wc), jnp.asarray(tw, dtype=F32), _hi_lo(wx), _hi_lo(wcx)


def _dot_hl(a_hl, x):
    xh = x.astype(BF16)
    xl = (x - xh.astype(F32)).astype(BF16)
    return _dot(a_hl[0], xh) + _dot(a_hl[0], xl) + _dot(a_hl[1], xh)


def _dot_lh(x, b_hl):
    xh = x.astype(BF16)
    xl = (x - xh.astype(F32)).astype(BF16)
    return _dot(xh, b_hl[0]) + _dot(xl, b_hl[0]) + _dot(xh, b_hl[1])


F_OCT = 8


def _fourier_kernel(x_ref, w1_ref, w2_ref, wc_ref, tw_ref, wx_ref, wcx_ref, o_ref, y1_ref):
    w1 = (w1_ref[0], w1_ref[1])
    w2 = (w2_ref[0], w2_ref[1])
    wc = (wc_ref[0], wc_ref[1])

    noct = FT2 // F_OCT
    for j in range(noct):
        cols = [x_ref[pl.ds(TC + F_OCT * j + q, FT1, stride=FT2), :] for q in range(F_OCT)]
        y = _dot_hl(w1, jnp.concatenate(cols, axis=1))
        for q in range(F_OCT):
            y1_ref[q, j * 2 * FT1:(j + 1) * 2 * FT1, :] = y[:, q * FGD:(q + 1) * FGD]

    def gather(kk):
        return jnp.concatenate([y1_ref[q, pl.ds(kk, noct, stride=2 * FT1), :] for q in range(F_OCT)], axis=0)

    cstep = tw_ref[0]
    sstep = tw_ref[1]

    def body(k1, carry):
        c, s = carry
        yr = gather(k1)
        yi = gather(FT1 + k1)
        y2 = jnp.concatenate([yr * c + yi * s, yi * c - yr * s], axis=0)
        y3 = _dot_hl(w2, y2)
        lhs = jnp.concatenate([y3[:FT2, :], y3[FT2:, :]], axis=1)
        o_ref[pl.ds(TC + k1, FT2, stride=FT1), :] = _dot_lh(lhs, wc)
        return c * cstep - s * sstep, s * cstep + c * sstep

    init = (jnp.ones((FT2, FGD), F32), jnp.zeros((FT2, FGD), F32))
    lax.fori_loop(0, FT1, body, init, unroll=4)

    ab = _dot_hl((wx_ref[0], wx_ref[1]), x_ref[0:TC, :])
    lhs = jnp.concatenate([ab[:TC, :], ab[TC:, :]], axis=1)
    o_ref[0:TC, :] = _dot_lh(lhs, (wcx_ref[0], wcx_ref[1]))


def _fourier(f_cat):
    w1, w2, wc, tw, wx, wcx = _fourier_tables()
    full = lambda a: pl.BlockSpec(a.shape, lambda b, g: (0,) * a.ndim)
    blk = pl.BlockSpec((None, TT, FGD), lambda b, g: (b, 0, g))
    return pl.pallas_call(
        _fourier_kernel,
        grid=(B, FG),
        in_specs=[blk, full(w1), full(w2), full(wc), full(tw), full(wx), full(wcx)],
        out_specs=blk,
        out_shape=jax.ShapeDtypeStruct((B, TT, FW), F32),
        scratch_shapes=[pltpu.VMEM((F_OCT, FT2 // F_OCT * 2 * FT1, FGD), F32)],
        compiler_params=_cp(("parallel", "parallel")),
        name="fourier_mix",
    )(f_cat, w1, w2, wc, tw, wx, wcx)


def _even_out_kernel(c_ref, l_ref, fo_ref, yf_ref, yb_ref, xs_ref, z_ref, gn_ref, ds_ref, wf_ref, wy_ref, g_ref,
                     mod_ref, o_ref):
    y = yf_ref[...].astype(F32) + yb_ref[...].astype(F32) + ds_ref[...] * xs_ref[...].astype(F32)
    gated = y * _silu(z_ref[...].astype(F32))
    gn = gn_ref[...]
    parts = []
    for g in range(SSD_G):
        parts.append(_rms(gated[:, g * GW:(g + 1) * GW], gn[:, g * GW:(g + 1) * GW]))
    yn = jnp.concatenate(parts, axis=1).astype(BF16)
    o = _dot(fo_ref[...].astype(BF16), wf_ref[...]) + _dot(yn, wy_ref[...])
    o_ref[...] = _stream_tile(c_ref, l_ref) + mod_ref[2:3, :] * _rms(o, g_ref[...])


def _even_out(ctx, x, fo, yf, yb, xbc_act, z, gn, dsum, wf, wy, g, modtab):
    row = lambda n: pl.BlockSpec((None, TM, n), lambda b, i: (b, i, 0))
    full = lambda a: pl.BlockSpec(a.shape, lambda b, i: (0,) * a.ndim)
    return pl.pallas_call(
        _even_out_kernel,
        grid=(B, NT_CAT),
        in_specs=_stream_specs() + [row(FW), row(SSD_INNER), row(SSD_INNER),
                                    row(SSD_INNER), row(SSD_INNER), full(gn), full(dsum), full(wf), full(wy),
                                    full(g), pl.BlockSpec((None, None, 8, D), _cat_mod_idx)],
        out_specs=row(D),
        out_shape=jax.ShapeDtypeStruct((B, TT, D), F32),
        compiler_params=_cp(("parallel", "parallel"), 56),
        name="even_out_proj",
    )(ctx, x, fo, yf, yb, xbc_act, z, gn, dsum, wf, wy, g, modtab)


def _ffn_kernel(s_ref, g2_ref, g3_ref, mod_ref, wg_ref, wu_ref, wd_ref, o_ref):
    a = _norm_mod_in(s_ref[...], g2_ref, mod_ref, 3).astype(BF16)
    gt = _dot(a, wg_ref[...])
    up = _dot(a, wu_ref[...])
    hid = (_silu(gt) * up).astype(BF16)
    f = _dot(hid, wd_ref[...])
    o_ref[...] = s_ref[...] + mod_ref[5:6, :] * _rms(f, g3_ref[...])


def _ffn(s, g2, g3, modtab, wg, wu, wd, cat):
    nt = NT_CAT if cat else NT_LAT
    rows = TT if cat else T
    row = pl.BlockSpec((None, TM, D), lambda b, i: (b, i, 0))
    full = lambda a: pl.BlockSpec(a.shape, lambda b, i: (0,) * a.ndim)
    return pl.pallas_call(
        _ffn_kernel,
        grid=(B, nt),
        in_specs=[row, full(g2), full(g3),
                  pl.BlockSpec((None, None, 8, D), _cat_mod_idx if cat else _lat_mod_idx),
                  full(wg), full(wu), full(wd)],
        out_specs=row,
        out_shape=jax.ShapeDtypeStruct((B, rows, D), F32),
        compiler_params=_cp(("parallel", "parallel"), 60),
        name="ffn_cat" if cat else "ffn_lat",
    )(s, g2, g3, modtab, wg, wu, wd)


def _rope_tables():
    rows = T // GRID_W
    row = jnp.repeat(jnp.arange(rows, dtype=F32), GRID_W)
    col = jnp.tile(jnp.arange(GRID_W, dtype=F32), rows)
    nf = QK // 4
    inv_freq = ROPE_BASE ** (-jnp.arange(nf, dtype=F32) * 2.0 / (QK // 2))
    ang = jnp.stack([row, col], axis=-1)[:, :, None] * inv_freq
    cos, sin = jnp.cos(ang), jnp.sin(ang)
    lane = np.arange(128)
    axis = (lane % QK) // (QK // 2)
    half = (lane % (QK // 2)) // nf
    freq = lane % nf
    cos_l = cos[:, axis, freq]
    sin_l = sin[:, axis, freq]
    sa = jnp.where(jnp.asarray(half == 0)[None, :], -sin_l, 0.0)
    sb = jnp.where(jnp.asarray(half == 1)[None, :], sin_l, 0.0)
    ctx1 = jnp.ones((TC, 128), F32)
    ctx0 = jnp.zeros((TC, 128), F32)
    return (jnp.concatenate([ctx1, cos_l], axis=0), jnp.concatenate([ctx0, sa], axis=0),
            jnp.concatenate([ctx0, sb], axis=0))


def _odd_in_kernel(s_ref, g_ref, mod_ref, wq_ref, wk_ref, wv_ref, wu_ref, cos_ref, sa_ref, sb_ref,
                   q_ref, k_ref, v_ref, u_ref):
    a = _norm_mod_in(s_ref[...], g_ref, mod_ref, 0).astype(BF16)
    cos = cos_ref[...]
    sa = sa_ref[...]
    sb = sb_ref[...]

    def rope(x, scale):
        outs = []
        for h in range(NH):
            xh = x[:, h * 128:(h + 1) * 128]
            r = xh * cos + pltpu.roll(xh, 112, 1) * sa + pltpu.roll(xh, 16, 1) * sb
            outs.append(r * scale if scale != 1.0 else r)
        return jnp.concatenate(outs, axis=1)

    q_ref[...] = rope(_dot(a, wq_ref[...]), ATTN_SCALE * math.log2(math.e)).astype(BF16)
    k_ref[...] = rope(_dot(a, wk_ref[...]), 1.0).astype(BF16)
    v_ref[...] = _dot(a, wv_ref[...]).astype(BF16)
    u_ref[...] = _dot(a, wu_ref[...]).astype(BF16)


def _odd_in(s, g, modtab, wq, wk, wv, wu, cos, sa, sb):
    row = lambda n: pl.BlockSpec((None, TM, n), lambda b, i: (b, i, 0))
    full = lambda a: pl.BlockSpec(a.shape, lambda b, i: (0,) * a.ndim)
    tab = pl.BlockSpec((TM, 128), lambda b, i: (i, 0))
    return pl.pallas_call(
        _odd_in_kernel,
        grid=(B, NT_CAT),
        in_specs=[row(D), full(g), pl.BlockSpec((None, None, 8, D), _cat_mod_idx),
                  full(wq), full(wk), full(wv), full(wu), tab, tab, tab],
        out_specs=[row(D), row(D), row(D), row(D)],
        out_shape=[jax.ShapeDtypeStruct((B, TT, D), BF16)] * 4,
        compiler_params=_cp(("parallel", "parallel"), 56),
        name="odd_in_proj",
    )(s, g, modtab, wq, wk, wv, wu, cos, sa, sb)


def _attn_kernel(*refs, lambda_init):
    nqb = TQ // TM
    q_refs = refs[:3 * nqb]
    k_ref, v_ref, lam_ref, sg_ref, o_ref, q2_ref, s_ref, mp_ref, mb_ref, ls_ref, acc_ref = refs[3 * nqb:]
    i = pl.program_id(2)
    lane_lo = lax.broadcasted_iota(jnp.int32, (TM, 128), 1) < QK

    for slot in range(3):
        for j in range(nqb):
            q = q_refs[slot * nqb + j][...]
            zero = jnp.zeros_like(q)
            q2_ref[slot, j * TM:(j + 1) * TM, :] = jnp.where(lane_lo, q, zero)
            q2_ref[slot, TQ + j * TM:TQ + (j + 1) * TM, :] = jnp.where(lane_lo, zero, q)

    def produce(c, qslot):
        r0 = pl.multiple_of(c * KT, KT)
        s = _dot_nt(q2_ref[qslot], k_ref[pl.ds(r0, KT), :])
        s_ref[c] = s
        mp_ref[...] = jnp.maximum(mp_ref[...], jnp.maximum(s[:, 0:128], s[:, 128:256]))

    def finish_max():
        m = jnp.max(mp_ref[...], axis=-1, keepdims=True)
        mb_ref[...] = jnp.broadcast_to(m, mb_ref.shape)
        mp_ref[...] = jnp.full(mp_ref.shape, -jnp.inf, F32)

    def consume(c):
        r0 = pl.multiple_of(c * KT, KT)
        s = s_ref[c]
        mb = mb_ref[...]
        p0 = jnp.exp2(s[:, 0:128] - mb)
        p1 = jnp.exp2(s[:, 128:256] - mb)
        ls_ref[...] += p0 + p1
        p = jnp.concatenate([p0, p1], axis=1).astype(BF16)
        acc_ref[...] += _dot(p, v_ref[pl.ds(r0, KT), :])

    lf = lam_ref[...]
    lam = (jnp.exp(jnp.sum(lf[0:1, :] * lf[1:2, :], axis=-1, keepdims=True))
           - jnp.exp(jnp.sum(lf[2:3, :] * lf[3:4, :], axis=-1, keepdims=True)) + lambda_init)

    def phase(next_q, out_rows):
        ls_ref[...] = jnp.zeros(ls_ref.shape, F32)
        acc_ref[...] = jnp.zeros(acc_ref.shape, F32)
        def body(c, carry):
            consume(c)
            produce(c, next_q)
            return carry

        lax.fori_loop(0, NKT, body, 0, unroll=11)
        finish_max()
        o = acc_ref[...] / jnp.sum(ls_ref[...], axis=-1, keepdims=True)
        od = o[0:TQ, :] - lam * o[TQ:2 * TQ, :]
        o_ref[out_rows, :] = (_rms(od, sg_ref[...]) * (1.0 - lambda_init)).astype(BF16)

    @pl.when(i == 0)
    def _():
        mp_ref[...] = jnp.full(mp_ref.shape, -jnp.inf, F32)

        def body(c, carry):
            produce(c, 0)
            return carry

        lax.fori_loop(0, NKT, body, 0)
        finish_max()

    phase(1, slice(0, TQ))
    phase(2, slice(TQ, 2 * TQ))


def _attention(q, k, v, lam, subln_g, lambda_init):
    nqb = TQ // TM
    nsteps = T // (2 * TQ)
    qspec = lambda j: pl.BlockSpec((None, TM, 128),
                                   lambda b, h, i: (b, jnp.minimum(2 * nqb * i + 1 + j, NT_LAT), h))
    return pl.pallas_call(
        functools.partial(_attn_kernel, lambda_init=lambda_init),
        grid=(B, NH, nsteps),
        in_specs=[qspec(j) for j in range(3 * nqb)] + [
                  pl.BlockSpec((None, TT, 128), lambda b, h, i: (b, 0, h)),
                  pl.BlockSpec((None, TT, 128), lambda b, h, i: (b, 0, h)),
                  pl.BlockSpec((4, QK), lambda b, h, i: (0, 0)),
                  pl.BlockSpec((1, DV), lambda b, h, i: (0, 0))],
        out_specs=pl.BlockSpec((None, 2 * TQ, 128), lambda b, h, i: (b, i, h)),
        out_shape=jax.ShapeDtypeStruct((B, T, NH * DV), BF16),
        scratch_shapes=[pltpu.VMEM((3, 2 * TQ, 128), BF16),
                        pltpu.VMEM((NKT, 2 * TQ, KT), F32),
                        pltpu.VMEM((2 * TQ, 128), F32),
                        pltpu.VMEM((2 * TQ, 128), F32),
                        pltpu.VMEM((2 * TQ, 128), F32),
                        pltpu.VMEM((2 * TQ, DV), F32)],
        compiler_params=_cp(("parallel", "parallel", "arbitrary"), 58),
        name="diff_attention",
    )(*([q] * (3 * nqb)), k, v, lam, subln_g)


CONF_HALO = 16


def _conf_conv_kernel(a_ref, gt_ref, w_ref, b_ref, o_ref, vs_ref, sh_ref):
    def fill(c, carry):
        r0 = pl.multiple_of(c * TM, TM)
        v = a_ref[pl.ds(r0, TM), :].astype(F32) * _sigmoid(gt_ref[pl.ds(r0, TM), :].astype(F32))
        vs_ref[pl.ds(r0, TM), :] = jnp.where(c >= 1, v, 0.0)
        return carry

    lax.fori_loop(0, NT_CAT, fill, 0)
    vs_ref[TT:TT + CONF_HALO, :] = jnp.zeros((CONF_HALO, 128), F32)
    w = w_ref[...]
    bias = b_ref[...]

    def body(c, carry):
        start = pl.multiple_of(c * TM + (TC - CONF_HALO), 8)
        win = vs_ref[pl.ds(start, TM + 2 * CONF_HALO), :]
        acc = jnp.broadcast_to(bias, (TM, 128))
        for s in range(8):
            sh_ref[...] = win[s:s + TM + CONF_HALO + 8, :]
            for k in range(CONF_K):
                off = CONF_HALO - CONF_K // 2 + k
                if off % 8 == s:
                    acc = acc + w[k:k + 1, :] * sh_ref[off - s:off - s + TM, :]
        o_ref[pl.ds(pl.multiple_of(c * TM, TM), TM), :] = acc
        return carry

    lax.fori_loop(0, NT_LAT, body, 0)


def _conf_conv(u, w32, bias):
    nct = CONF_C // 128
    return pl.pallas_call(
        _conf_conv_kernel,
        grid=(B, nct),
        in_specs=[pl.BlockSpec((None, TT, 128), lambda b, c: (b, 0, c)),
                  pl.BlockSpec((None, TT, 128), lambda b, c: (b, 0, nct + c)),
                  pl.BlockSpec((32, 128), lambda b, c: (0, c)),
                  pl.BlockSpec((1, 128), lambda b, c: (0, c))],
        out_specs=pl.BlockSpec((None, T, 128), lambda b, c: (b, 0, c)),
        out_shape=jax.ShapeDtypeStruct((B, T, CONF_C), F32),
        scratch_shapes=[pltpu.VMEM((TT + CONF_HALO, 128), F32),
                        pltpu.VMEM((TM + CONF_HALO + 8, 128), F32)],
        compiler_params=_cp(("parallel", "parallel")),
        name="conformer_conv31",
    )(u, u, w32, bias)


def _odd_out_kernel(s_ref, oa_ref, cv_ref, cg_ref, cb_ref, wa_ref, wc_ref, g_ref, mod_ref, o_ref):
    v = cv_ref[...]
    mu = jnp.mean(v, axis=-1, keepdims=True)
    vc = v - mu
    var = jnp.mean(vc * vc, axis=-1, keepdims=True)
    cv = _silu(vc * lax.rsqrt(var + EPS) * cg_ref[...] + cb_ref[...])
    o = _dot(oa_ref[...], wa_ref[...]) + _dot(cv.astype(BF16), wc_ref[...])
    o_ref[...] = s_ref[...] + mod_ref[2:3, :] * _rms(o, g_ref[...])


def _odd_out(s_cat, oa, cvpre, cg, cb, wa, wc, g, modtab):
    full = lambda a: pl.BlockSpec(a.shape, lambda b, i: (0,) * a.ndim)
    row = lambda n: pl.BlockSpec((None, TM, n), lambda b, i: (b, i, 0))
    return pl.pallas_call(
        _odd_out_kernel,
        grid=(B, NT_LAT),
        in_specs=[pl.BlockSpec((None, TM, D), lambda b, i: (b, i + 1, 0)), row(NH * DV), row(CONF_C),
                  full(cg), full(cb), full(wa), full(wc), full(g),
                  pl.BlockSpec((None, None, 8, D), _lat_mod_idx)],
        out_specs=row(D),
        out_shape=jax.ShapeDtypeStruct((B, T, D), F32),
        compiler_params=_cp(("parallel", "parallel")),
        name="odd_out_proj",
    )(s_cat, oa, cvpre, cg, cb, wa, wc, g, modtab)


def _pad_rows(a, n):
    return jnp.pad(a, ((0, n - a.shape[0]), (0, 0)))


def _pad_lanes(a, n):
    return jnp.pad(a, ((0, 0), (0, n - a.shape[1])))


def kernel(x, c, ctx, c_ctx, mod_w, mod_b, norm_g, ffn_w_gate, ffn_w_up, ffn_w_down, ev_w_in, ev_conv_w, ev_conv_b, ev_dt_bias, ev_a_log, ev_d_skip, ev_gnorm_g, ev_w_out, od_w_in, od_lambda, od_subln_g, od_conv_w, od_conv_b, od_cnorm_g, od_cnorm_b, od_w_out):
    bf = lambda a: a.astype(BF16)

    cvec = _pad_rows(jnp.concatenate([c, c_ctx[None, :]], axis=0), 8)
    mv = _mod_vectors(cvec, mod_w, mod_b).reshape(DEPTH, 8, N_MOD, D)
    modtabs = []
    for l in range(DEPTH):
        lat = mv[l, :B]
        cx = jnp.broadcast_to(mv[l, B][None], (B, N_MOD, D))
        tab = jnp.stack([cx, lat], axis=1)
        modtabs.append(jnp.pad(tab, ((0, 0), (0, 0), (0, 8 - N_MOD), (0, 0))))

    g = norm_g[0]
    w_in = ev_w_in[0]
    o0, o1, o2 = FW, FW + SSD_INNER, FW + SSD_INNER + SSD_XBC
    f, z, xbc, dt = _even_in(ctx, x, g[0:1], modtabs[0], bf(w_in[:, :o0]), bf(w_in[:, o0:o1]),
                             bf(w_in[:, o1:o2]), bf(_pad_lanes(w_in[:, o2:], DT_PAD)))
    xbc_act = _conv5(xbc, _pad_rows(ev_conv_w[0], 8), ev_conv_b[0][None, :])
    bias_all = _pad_lanes(ev_dt_bias[0].reshape(1, -1), DT_PAD)
    alog_all = _pad_lanes(ev_a_log[0].reshape(1, -1), DT_PAD)
    yf, yb = _ssd(xbc_act, dt, bias_all, alog_all)
    fo = _fourier(f)
    dsum = jnp.repeat(ev_d_skip[0, 0] + ev_d_skip[0, 1], SSD_P)[None, :]
    w_out = ev_w_out[0]
    s = _even_out(ctx, x, fo, yf, yb, xbc_act, z, ev_gnorm_g[0][None, :], dsum, bf(w_out[:FW]), bf(w_out[FW:]),
                  g[1:2], modtabs[0])
    s = _ffn(s, g[2:3], g[3:4], modtabs[0], bf(ffn_w_gate[0]), bf(ffn_w_up[0]), bf(ffn_w_down[0]), True)

    g = norm_g[1]
    lambda_init = 0.8 - 0.6 * math.exp(-0.3 * 1)
    w_in = od_w_in[0]
    cos, sa, sb = _rope_tables()
    q, k, v, u = _odd_in(s, g[0:1], modtabs[1], bf(w_in[:, :D]), bf(w_in[:, D:2 * D]),
                         bf(w_in[:, 2 * D:3 * D]), bf(w_in[:, 3 * D:]), cos, sa, sb)
    oa = _attention(q, k, v, od_lambda[0], od_subln_g[0][None, :], lambda_init)
    cvpre = _conf_conv(u, _pad_rows(od_conv_w[0], 32), od_conv_b[0][None, :])
    w_out = od_w_out[0]
    h = _odd_out(s, oa, cvpre, od_cnorm_g[0][None, :], od_cnorm_b[0][None, :], bf(w_out[:NH * DV]),
                 bf(w_out[NH * DV:]), g[1:2], modtabs[1])
    return _ffn(h, g[2:3], g[3:4], modtabs[1], bf(ffn_w_gate[1]), bf(ffn_w_up[1]), bf(ffn_w_down[1]), False)
```

```python
import functools
import math

import numpy as np
import jax
import jax.numpy as jnp
from jax import lax
from jax.experimental import pallas as pl
from jax.experimental.pallas import tpu as pltpu

F32 = jnp.float32
BF16 = jnp.bfloat16
HIGHEST = lax.Precision.HIGHEST

D = 1024
B = 2
T = 8192
TC = 256
TT = TC + T
DEPTH = 2
N_MOD = 6
HID = 2816
EPS = 1e-6
GRID_W = 64

TM = 256
NT_CAT = TT // TM
NT_LAT = T // TM

FW = 512
FG = 4
FGD = 128
SSD_INNER = 1536
SSD_G = 4
SSD_HPG = 6
SSD_P = 64
SSD_N = 128
SSD_L = 128
SSD_XBC = 2560
SSD_K = 5
NCH_CAT = TT // SSD_L
NCH_CTX = TC // SSD_L
GW = SSD_HPG * SSD_P
DT_PAD = 128

NH = 8
QK = 64
DV = 128
ATTN_SCALE = QK ** -0.5
CONF_C = 512
CONF_K = 31
ROPE_BASE = 10000.0
TQ = 512
KT = 256
NKT = TT // KT

FT1 = 64
FT2 = 128


def _cp(sem, vmem_mb=48):
    return pltpu.CompilerParams(dimension_semantics=sem, vmem_limit_bytes=vmem_mb * 1024 * 1024)


def _sigmoid(x):
    return 1.0 / (1.0 + jnp.exp(-x))


def _silu(x):
    return x * _sigmoid(x)


def _rms(x, g):
    ms = jnp.mean(x * x, axis=-1, keepdims=True)
    return x * lax.rsqrt(ms + EPS) * g


def _dot(a, b):
    return jnp.dot(a, b, preferred_element_type=F32)


def _dot_nt(a, b):
    return lax.dot_general(a, b, (((1,), (1,)), ((), ())), preferred_element_type=F32)


def _split3(x):
    hi = x.astype(BF16)
    r1 = x - hi.astype(F32)
    mid = r1.astype(BF16)
    lo = (r1 - mid.astype(F32)).astype(BF16)
    return hi, mid, lo


def _dot3_l(m01, x):
    hi, mid, lo = _split3(x)
    return _dot(m01, hi) + _dot(m01, mid) + _dot(m01, lo)


def _dot3_r(x, m01):
    hi, mid, lo = _split3(x)
    return _dot(hi, m01) + _dot(mid, m01) + _dot(lo, m01)


def _mod_kernel(c_ref, w_ref, b_ref, o_ref):
    a = _silu(c_ref[...])
    o_ref[...] = jnp.dot(a, w_ref[...], precision=HIGHEST, preferred_element_type=F32) + b_ref[...]


def _mod_vectors(cvec, mod_w, mod_b):
    tn = 1536
    return pl.pallas_call(
        _mod_kernel,
        grid=(DEPTH, N_MOD * D // tn),
        in_specs=[
            pl.BlockSpec((8, D), lambda l, j: (0, 0)),
            pl.BlockSpec((None, D, tn), lambda l, j: (l, 0, j)),
            pl.BlockSpec((None, 1, tn), lambda l, j: (l, 0, j)),
        ],
        out_specs=pl.BlockSpec((None, 8, tn), lambda l, j: (l, 0, j)),
        out_shape=jax.ShapeDtypeStruct((DEPTH, 8, N_MOD * D), F32),
        compiler_params=_cp(("parallel", "parallel")),
        name="mod_vectors",
    )(cvec, mod_w, mod_b.reshape(DEPTH, 1, N_MOD * D))


def _cat_mod_idx(b, i):
    return (b, jnp.minimum(i, 1), 0, 0)


def _lat_mod_idx(b, i):
    return (b, 1, 0, 0)


def _norm_mod_in(x, g_ref, mod_ref, row):
    y = _rms(x, g_ref[...])
    return y * (1.0 + mod_ref[row + 1:row + 2, :]) + mod_ref[row:row + 1, :]


def _stream_tile(ctx_ref, lat_ref):
    return jnp.where(pl.program_id(1) == 0, ctx_ref[...], lat_ref[...])


def _stream_specs():
    return [pl.BlockSpec((None, TM, D), lambda b, i: (b, 0, 0)),
            pl.BlockSpec((None, TM, D), lambda b, i: (b, jnp.maximum(i - 1, 0), 0))]


def _even_in_kernel(c_ref, l_ref, g_ref, mod_ref, wf_ref, wz_ref, wx_ref, wd_ref, f_ref, z_ref, x_ref, dt_ref):
    a = _norm_mod_in(_stream_tile(c_ref, l_ref), g_ref, mod_ref, 0).astype(BF16)
    f_ref[...] = _dot(a, wf_ref[...])
    z_ref[...] = _dot(a, wz_ref[...]).astype(BF16)
    x_ref[...] = _dot(a, wx_ref[...]).astype(BF16)
    dt_ref[...] = _dot(a, wd_ref[...])


def _even_in(ctx, x, g, modtab, wf, wz, wx, wd):
    row = lambda n: pl.BlockSpec((None, TM, n), lambda b, i: (b, i, 0))
    full = lambda a: pl.BlockSpec(a.shape, lambda b, i: (0,) * a.ndim)
    return pl.pallas_call(
        _even_in_kernel,
        grid=(B, NT_CAT),
        in_specs=_stream_specs() + [full(g), pl.BlockSpec((None, None, 8, D), _cat_mod_idx),
                                    full(wf), full(wz), full(wx), full(wd)],
        out_specs=[row(FW), row(SSD_INNER), row(SSD_XBC), row(DT_PAD)],
        out_shape=[jax.ShapeDtypeStruct((B, TT, n), dt) for n, dt in
                   ((FW, F32), (SSD_INNER, BF16), (SSD_XBC, BF16), (DT_PAD, F32))],
        compiler_params=_cp(("parallel", "parallel"), 56),
        name="even_in_proj",
    )(ctx, x, g, modtab, wf, wz, wx, wd)


CONV_HALO = 16


def _conv5_kernel(x_ref, w_ref, b_ref, o_ref):
    w = w_ref[...]
    bias = b_ref[...]

    def body(c, carry):
        r0 = pl.multiple_of(c * TM, TM)
        center = x_ref[pl.ds(r0, TM), :].astype(F32)
        lo = pl.multiple_of(jnp.maximum(r0 - CONV_HALO, 0), CONV_HALO)
        hi = pl.multiple_of(jnp.minimum(r0 + TM, TT - CONV_HALO), CONV_HALO)
        left = jnp.where(c >= 2, x_ref[pl.ds(lo, CONV_HALO), :].astype(F32), 0.0)
        right = jnp.where(jnp.logical_and(c >= 1, c < NT_CAT - 1),
                          x_ref[pl.ds(hi, CONV_HALO), :].astype(F32), 0.0)
        ext = jnp.concatenate([left, center, right], axis=0)
        acc = bias + w[2:3, :] * center
        for k in (0, 1, 3, 4):
            off = CONV_HALO - SSD_K // 2 + k
            acc = acc + w[k:k + 1, :] * ext[off:off + TM, :]
        o_ref[pl.ds(r0, TM), :] = _silu(acc).astype(BF16)
        return carry

    lax.fori_loop(0, NT_CAT, body, 0)


def _conv5(xbc, w8, bias):
    return pl.pallas_call(
        _conv5_kernel,
        grid=(B, SSD_XBC // 128),
        in_specs=[
            pl.BlockSpec((None, TT, 128), lambda b, c: (b, 0, c)),
            pl.BlockSpec((8, 128), lambda b, c: (0, c)),
            pl.BlockSpec((1, 128), lambda b, c: (0, c)),
        ],
        out_specs=pl.BlockSpec((None, TT, 128), lambda b, c: (b, 0, c)),
        out_shape=jax.ShapeDtypeStruct((B, TT, SSD_XBC), BF16),
        compiler_params=_cp(("parallel", "parallel")),
        name="ssd_conv5",
    )(xbc, w8, bias)


def _ssd_bwd_chunk(i):
    return jnp.where(i < NCH_CTX, NCH_CTX - 1 - i, NCH_CAT + NCH_CTX - 1 - i)


SSD_NH = SSD_G * SSD_HPG


def _ssd_expand_table():
    lane = np.arange(DT_PAD)[:, None]
    col = np.arange(SSD_INNER)[None, :]
    tabs = []
    for d in range(2):
        e = (lane == d * SSD_NH + col // SSD_P).astype(np.float32)
        tabs.append(np.concatenate([e, e], axis=0))
    return jnp.asarray(np.stack(tabs), dtype=BF16)


def _hi_mid(x):
    hi = x.astype(BF16)
    mid = (x - hi.astype(F32)).astype(BF16)
    return jnp.concatenate([hi, mid], axis=1)


def _ssd_kernel(xf_ref, bf_ref, cf_ref, dtf_ref, xb_ref, bb_ref, cb_ref, dtb_ref, bias_ref, alog_ref,
                ep_ref, yf_ref, yb_ref, st_ref):
    @pl.when(pl.program_id(1) == 0)
    def _():
        st_ref[...] = jnp.zeros_like(st_ref)

    row = lax.broadcasted_iota(jnp.int32, (SSD_L, SSD_L), 0)
    col = lax.broadcasted_iota(jnp.int32, (SSD_L, SSD_L), 1)
    lane_lo = lax.broadcasted_iota(jnp.int32, (SSD_L, 128), 1) < SSD_P
    bias = bias_ref[...]
    a_all = -jnp.exp(alog_ref[...]) * math.log2(math.e)
    dirs = ((xf_ref, bf_ref, cf_ref, dtf_ref, yf_ref), (xb_ref, bb_ref, cb_ref, dtb_ref, yb_ref))

    for d, (xs_ref, bm_ref, cm_ref, dt_ref, y_ref) in enumerate(dirs):
        tri = (row >= col) if d == 0 else (row <= col)
        tri_b = jnp.where(tri, 1.0, 0.0).astype(BF16)
        raw = dt_ref[...] + bias
        dt_all = jnp.maximum(raw, 0.0) + jnp.log1p(jnp.exp(-jnp.abs(raw)))
        acs = _dot3_l(tri_b, dt_all * a_all)
        acs_t = acs.T
        dt_e = _dot(_hi_mid(dt_all), ep_ref[d])
        acs_e = _dot(_hi_mid(acs), ep_ref[d])
        last = SSD_L - 1 if d == 0 else 0
        tot_e = acs_e[last:last + 1, :]

        def head_mat(cb, h):
            c = d * SSD_NH + h
            seg = acs[:, c:c + 1] - acs_t[c:c + 1, :]
            dec = jnp.exp2(jnp.where(tri, seg, -jnp.inf))
            return (cb * dec).astype(BF16)

        for g in range(SSD_G):
            cm_b = cm_ref[:, g * SSD_N:(g + 1) * SSD_N]
            bm = bm_ref[:, g * SSD_N:(g + 1) * SSD_N]
            cb = _dot_nt(cm_b, bm)
            bm_t = bm.astype(F32).T.astype(BF16)
            glanes = slice(g * GW, (g + 1) * GW)
            acs_g = acs_e[:, glanes]
            tot_g = tot_e[:, glanes]
            xdt = xs_ref[:, glanes].astype(F32) * dt_e[:, glanes]
            st_old = st_ref[d, g]
            y_off = _dot(cm_b, st_old.astype(BF16)) * jnp.exp2(acs_g)
            st_ref[d, g] = st_old * jnp.exp2(tot_g) + _dot(bm_t, (xdt * jnp.exp2(tot_g - acs_g)).astype(BF16))
            for j in range(SSD_HPG // 2):
                h0 = g * SSD_HPG + 2 * j
                x_p = xdt[:, j * 128:(j + 1) * 128]
                x2 = jnp.concatenate([jnp.where(lane_lo, x_p, 0.0), jnp.where(lane_lo, 0.0, x_p)], axis=0)
                m2 = jnp.concatenate([head_mat(cb, h0), head_mat(cb, h0 + 1)], axis=1)
                y_diag = _dot(m2, x2.astype(BF16))
                lanes = slice(g * GW + j * 128, g * GW + (j + 1) * 128)
                y_ref[:, lanes] = (y_diag + y_off[:, j * 128:(j + 1) * 128]).astype(BF16)


def _ssd(xbc_act, dt, bias_all, alog_all):
    nbc = SSD_INNER // (SSD_G * SSD_N)
    ep = _ssd_expand_table()
    fwd = lambda blk: (lambda b, i: (b, i, blk))
    bwd = lambda blk: (lambda b, i: (b, _ssd_bwd_chunk(i), blk))
    specs = lambda ix: [pl.BlockSpec((None, SSD_L, SSD_INNER), ix(0)),
                        pl.BlockSpec((None, SSD_L, SSD_G * SSD_N), ix(nbc)),
                        pl.BlockSpec((None, SSD_L, SSD_G * SSD_N), ix(nbc + 1)),
                        pl.BlockSpec((None, SSD_L, DT_PAD), ix(0))]
    par = pl.BlockSpec((1, DT_PAD), lambda b, i: (0, 0))
    full = lambda a: pl.BlockSpec(a.shape, lambda b, i: (0,) * a.ndim)
    return pl.pallas_call(
        _ssd_kernel,
        grid=(B, NCH_CAT),
        in_specs=specs(fwd) + specs(bwd) + [par, par, full(ep)],
        out_specs=[pl.BlockSpec((None, SSD_L, SSD_INNER), fwd(0)),
                   pl.BlockSpec((None, SSD_L, SSD_INNER), bwd(0))],
        out_shape=[jax.ShapeDtypeStruct((B, TT, SSD_INNER), BF16)] * 2,
        scratch_shapes=[pltpu.VMEM((2, SSD_G, SSD_N, GW), F32)],
        compiler_params=_cp(("parallel", "arbitrary")),
        name="ssd_scan",
    )(xbc_act, xbc_act, xbc_act, dt, xbc_act, xbc_act, xbc_act, dt, bias_all, alog_all, ep)


def _dft_cs(n):
    k = np.arange(n, dtype=np.int64)
    ang = 2.0 * np.pi * ((k[:, None] * k[None, :]) % n).astype(np.float64) / n
    return np.cos(ang), np.sin(ang)


def _hi_lo(a):
    a32 = jnp.asarray(a, dtype=F32)
    hi = a32.astype(BF16)
    lo = (a32 - hi.astype(F32)).astype(BF16)
    return jnp.stack([hi, lo], axis=0)


def _fourier_tables():
    c1, s1 = _dft_cs(FT1)
    w1 = np.concatenate([c1, -s1], axis=0)
    r = np.arange(FT2)
    t2_of_r = F_OCT * (r % (FT2 // F_OCT)) + r // (FT2 // F_OCT)
    c2, s2 = _dft_cs(FT2)
    c2, s2 = c2[:, t2_of_r], s2[:, t2_of_r]
    w2 = np.block([[c2, s2], [-s2, c2]])
    cc, sc = _dft_cs(FGD)
    wc = np.concatenate([cc, sc], axis=0) / math.sqrt(T * FGD)
    ang = 2.0 * np.pi * t2_of_r.astype(np.float64)[:, None] / T
    tw = np.stack([np.cos(ang), np.sin(ang)], axis=0) * np.ones((1, 1, FGD))
    cx, sx = _dft_cs(TC)
    wx = np.concatenate([cx, sx], axis=0)
    wcx = np.concatenate([cc, -sc], axis=0) / math.sqrt(TC * FGD)
    return _hi_lo(w1), _hi_lo(w2), _hi_lo(wc), jnp.asarray(tw, dtype=F32), _hi_lo(wx), _hi_lo(wcx)


def _dot_hl(a_hl, x):
    xh = x.astype(BF16)
    xl = (x - xh.astype(F32)).astype(BF16)
    return _dot(a_hl[0], xh) + _dot(a_hl[0], xl) + _dot(a_hl[1], xh)


def _dot_lh(x, b_hl):
    xh = x.astype(BF16)
    xl = (x - xh.astype(F32)).astype(BF16)
    return _dot(xh, b_hl[0]) + _dot(xl, b_hl[0]) + _dot(xh, b_hl[1])


F_OCT = 8


def _fourier_kernel(x_ref, w1_ref, w2_ref, wc_ref, tw_ref, wx_ref, wcx_ref, o_ref, y1_ref):
    w1 = (w1_ref[0], w1_ref[1])
    w2 = (w2_ref[0], w2_ref[1])
    wc = (wc_ref[0], wc_ref[1])

    noct = FT2 // F_OCT
    for j in range(noct):
        cols = [x_ref[pl.ds(TC + F_OCT * j + q, FT1, stride=FT2), :] for q in range(F_OCT)]
        y = _dot_hl(w1, jnp.concatenate(cols, axis=1))
        for q in range(F_OCT):
            y1_ref[q, j * 2 * FT1:(j + 1) * 2 * FT1, :] = y[:, q * FGD:(q + 1) * FGD]

    def gather(kk):
        return jnp.concatenate([y1_ref[q, pl.ds(kk, noct, stride=2 * FT1), :] for q in range(F_OCT)], axis=0)

    cstep = tw_ref[0]
    sstep = tw_ref[1]

    def rotate(c, s):
        return c * cstep - s * sstep, s * cstep + c * sstep

    def twiddled(k1, c, s):
        yr = gather(k1)
        yi = gather(FT1 + k1)
        return yr * c + yi * s, yi * c - yr * s

    def body(i, carry):
        c0, s0 = carry
        c1, s1 = rotate(c0, s0)
        k1 = 2 * i
        re0, im0 = twiddled(k1, c0, s0)
        re1, im1 = twiddled(k1 + 1, c1, s1)
        y2 = jnp.concatenate([jnp.concatenate([re0, re1], axis=1),
                              jnp.concatenate([im0, im1], axis=1)], axis=0)
        y3 = _dot_hl(w2, y2)
        lhs = jnp.concatenate([jnp.concatenate([y3[:FT2, :FGD], y3[FT2:, :FGD]], axis=1),
                               jnp.concatenate([y3[:FT2, FGD:], y3[FT2:, FGD:]], axis=1)], axis=0)
        res = _dot_lh(lhs, wc)
        o_ref[pl.ds(TC + k1, FT2, stride=FT1), :] = res[:FT2, :]
        o_ref[pl.ds(TC + k1 + 1, FT2, stride=FT1), :] = res[FT2:, :]
        return rotate(c1, s1)

    init = (jnp.ones((FT2, FGD), F32), jnp.zeros((FT2, FGD), F32))
    lax.fori_loop(0, FT1 // 2, body, init, unroll=8)

    ab = _dot_hl((wx_ref[0], wx_ref[1]), x_ref[0:TC, :])
    lhs = jnp.concatenate([ab[:TC, :], ab[TC:, :]], axis=1)
    o_ref[0:TC, :] = _dot_lh(lhs, (wcx_ref[0], wcx_ref[1]))


def _fourier(f_cat):
    w1, w2, wc, tw, wx, wcx = _fourier_tables()
    full = lambda a: pl.BlockSpec(a.shape, lambda b, g: (0,) * a.ndim)
    blk = pl.BlockSpec((None, TT, FGD), lambda b, g: (b, 0, g))
    return pl.pallas_call(
        _fourier_kernel,
        grid=(B, FG),
        in_specs=[blk, full(w1), full(w2), full(wc), full(tw), full(wx), full(wcx)],
        out_specs=blk,
        out_shape=jax.ShapeDtypeStruct((B, TT, FW), F32),
        scratch_shapes=[pltpu.VMEM((F_OCT, FT2 // F_OCT * 2 * FT1, FGD), F32)],
        compiler_params=_cp(("parallel", "parallel")),
        name="fourier_mix",
    )(f_cat, w1, w2, wc, tw, wx, wcx)


def _even_out_kernel(c_ref, l_ref, fo_ref, yf_ref, yb_ref, xs_ref, z_ref, gn_ref, ds_ref, wf_ref, wy_ref, g_ref,
                     mod_ref, o_ref):
    y = yf_ref[...].astype(F32) + yb_ref[...].astype(F32) + ds_ref[...] * xs_ref[...].astype(F32)
    gated = y * _silu(z_ref[...].astype(F32))
    gn = gn_ref[...]
    parts = []
    for g in range(SSD_G):
        parts.append(_rms(gated[:, g * GW:(g + 1) * GW], gn[:, g * GW:(g + 1) * GW]))
    yn = jnp.concatenate(parts, axis=1).astype(BF16)
    o = _dot(fo_ref[...].astype(BF16), wf_ref[...]) + _dot(yn, wy_ref[...])
    o_ref[...] = _stream_tile(c_ref, l_ref) + mod_ref[2:3, :] * _rms(o, g_ref[...])


def _even_out(ctx, x, fo, yf, yb, xbc_act, z, gn, dsum, wf, wy, g, modtab):
    row = lambda n: pl.BlockSpec((None, TM, n), lambda b, i: (b, i, 0))
    full = lambda a: pl.BlockSpec(a.shape, lambda b, i: (0,) * a.ndim)
    return pl.pallas_call(
        _even_out_kernel,
        grid=(B, NT_CAT),
        in_specs=_stream_specs() + [row(FW), row(SSD_INNER), row(SSD_INNER),
                                    row(SSD_INNER), row(SSD_INNER), full(gn), full(dsum), full(wf), full(wy),
                                    full(g), pl.BlockSpec((None, None, 8, D), _cat_mod_idx)],
        out_specs=row(D),
        out_shape=jax.ShapeDtypeStruct((B, TT, D), F32),
        compiler_params=_cp(("parallel", "parallel"), 56),
        name="even_out_proj",
    )(ctx, x, fo, yf, yb, xbc_act, z, gn, dsum, wf, wy, g, modtab)


def _ffn_kernel(s_ref, g2_ref, g3_ref, mod_ref, wg_ref, wu_ref, wd_ref, o_ref):
    a = _norm_mod_in(s_ref[...], g2_ref, mod_ref, 3).astype(BF16)
    gt = _dot(a, wg_ref[...])
    up = _dot(a, wu_ref[...])
    hid = (_silu(gt) * up).astype(BF16)
    f = _dot(hid, wd_ref[...])
    o_ref[...] = s_ref[...] + mod_ref[5:6, :] * _rms(f, g3_ref[...])


def _ffn(s, g2, g3, modtab, wg, wu, wd, cat):
    nt = NT_CAT if cat else NT_LAT
    rows = TT if cat else T
    row = pl.BlockSpec((None, TM, D), lambda b, i: (b, i, 0))
    full = lambda a: pl.BlockSpec(a.shape, lambda b, i: (0,) * a.ndim)
    return pl.pallas_call(
        _ffn_kernel,
        grid=(B, nt),
        in_specs=[row, full(g2), full(g3),
                  pl.BlockSpec((None, None, 8, D), _cat_mod_idx if cat else _lat_mod_idx),
                  full(wg), full(wu), full(wd)],
        out_specs=row,
        out_shape=jax.ShapeDtypeStruct((B, rows, D), F32),
        compiler_params=_cp(("parallel", "parallel"), 60),
        name="ffn_cat" if cat else "ffn_lat",
    )(s, g2, g3, modtab, wg, wu, wd)


def _rope_tables():
    rows = T // GRID_W
    row = jnp.repeat(jnp.arange(rows, dtype=F32), GRID_W)
    col = jnp.tile(jnp.arange(GRID_W, dtype=F32), rows)
    nf = QK // 4
    inv_freq = ROPE_BASE ** (-jnp.arange(nf, dtype=F32) * 2.0 / (QK // 2))
    ang = jnp.stack([row, col], axis=-1)[:, :, None] * inv_freq
    cos, sin = jnp.cos(ang), jnp.sin(ang)
    lane = np.arange(128)
    axis = (lane % QK) // (QK // 2)
    half = (lane % (QK // 2)) // nf
    freq = lane % nf
    cos_l = cos[:, axis, freq]
    sin_l = sin[:, axis, freq]
    sa = jnp.where(jnp.asarray(half == 0)[None, :], -sin_l, 0.0)
    sb = jnp.where(jnp.asarray(half == 1)[None, :], sin_l, 0.0)
    ctx1 = jnp.ones((TC, 128), F32)
    ctx0 = jnp.zeros((TC, 128), F32)
    return (jnp.concatenate([ctx1, cos_l], axis=0), jnp.concatenate([ctx0, sa], axis=0),
            jnp.concatenate([ctx0, sb], axis=0))


def _odd_in_kernel(s_ref, g_ref, mod_ref, wq_ref, wk_ref, wv_ref, wu_ref, cos_ref, sa_ref, sb_ref,
                   q_ref, k_ref, v_ref, u_ref):
    a = _norm_mod_in(s_ref[...], g_ref, mod_ref, 0).astype(BF16)
    cos = cos_ref[...]
    sa = sa_ref[...]
    sb = sb_ref[...]

    def rope(x, scale):
        outs = []
        for h in range(NH):
            xh = x[:, h * 128:(h + 1) * 128]
            r = xh * cos + pltpu.roll(xh, 112, 1) * sa + pltpu.roll(xh, 16, 1) * sb
            outs.append(r * scale if scale != 1.0 else r)
        return jnp.concatenate(outs, axis=1)

    q_ref[...] = rope(_dot(a, wq_ref[...]), ATTN_SCALE * math.log2(math.e)).astype(BF16)
    k_ref[...] = rope(_dot(a, wk_ref[...]), 1.0).astype(BF16)
    v_ref[...] = _dot(a, wv_ref[...]).astype(BF16)
    u_ref[...] = _dot(a, wu_ref[...]).astype(BF16)


def _odd_in(s, g, modtab, wq, wk, wv, wu, cos, sa, sb):
    row = lambda n: pl.BlockSpec((None, TM, n), lambda b, i: (b, i, 0))
    full = lambda a: pl.BlockSpec(a.shape, lambda b, i: (0,) * a.ndim)
    tab = pl.BlockSpec((TM, 128), lambda b, i: (i, 0))
    return pl.pallas_call(
        _odd_in_kernel,
        grid=(B, NT_CAT),
        in_specs=[row(D), full(g), pl.BlockSpec((None, None, 8, D), _cat_mod_idx),
                  full(wq), full(wk), full(wv), full(wu), tab, tab, tab],
        out_specs=[row(D), row(D), row(D), row(D)],
        out_shape=[jax.ShapeDtypeStruct((B, TT, D), BF16)] * 4,
        compiler_params=_cp(("parallel", "parallel"), 56),
        name="odd_in_proj",
    )(s, g, modtab, wq, wk, wv, wu, cos, sa, sb)


def _attn_kernel(*refs, lambda_init):
    nqb = TQ // TM
    q_refs = refs[:3 * nqb]
    k_ref, v_ref, lam_ref, sg_ref, o_ref, q2_ref, s_ref, mp_ref, mb_ref, ls_ref, acc_ref = refs[3 * nqb:]
    i = pl.program_id(2)
    lane_lo = lax.broadcasted_iota(jnp.int32, (TM, 128), 1) < QK

    for slot in range(3):
        for j in range(nqb):
            q = q_refs[slot * nqb + j][...]
            zero = jnp.zeros_like(q)
            q2_ref[slot, j * TM:(j + 1) * TM, :] = jnp.where(lane_lo, q, zero)
            q2_ref[slot, TQ + j * TM:TQ + (j + 1) * TM, :] = jnp.where(lane_lo, zero, q)

    def produce(c, qslot):
        r0 = pl.multiple_of(c * KT, KT)
        s = _dot_nt(q2_ref[qslot], k_ref[pl.ds(r0, KT), :])
        s_ref[c] = s
        mp_ref[...] = jnp.maximum(mp_ref[...], jnp.maximum(s[:, 0:128], s[:, 128:256]))

    def finish_max():
        m = jnp.max(mp_ref[...], axis=-1, keepdims=True)
        mb_ref[...] = jnp.broadcast_to(m, mb_ref.shape)
        mp_ref[...] = jnp.full(mp_ref.shape, -jnp.inf, F32)

    def consume(c):
        r0 = pl.multiple_of(c * KT, KT)
        s = s_ref[c]
        mb = mb_ref[...]
        p0 = jnp.exp2(s[:, 0:128] - mb)
        p1 = jnp.exp2(s[:, 128:256] - mb)
        ls_ref[...] += p0 + p1
        p = jnp.concatenate([p0, p1], axis=1).astype(BF16)
        acc_ref[...] += _dot(p, v_ref[pl.ds(r0, KT), :])

    lf = lam_ref[...]
    lam = (jnp.exp(jnp.sum(lf[0:1, :] * lf[1:2, :], axis=-1, keepdims=True))
           - jnp.exp(jnp.sum(lf[2:3, :] * lf[3:4, :], axis=-1, keepdims=True)) + lambda_init)

    def phase(next_q, out_rows):
        ls_ref[...] = jnp.zeros(ls_ref.shape, F32)
        acc_ref[...] = jnp.zeros(acc_ref.shape, F32)
        def body(c, carry):
            consume(c)
            produce(c, next_q)
            return carry

        lax.fori_loop(0, NKT, body, 0, unroll=11)
        finish_max()
        o = acc_ref[...] / jnp.sum(ls_ref[...], axis=-1, keepdims=True)
        od = o[0:TQ, :] - lam * o[TQ:2 * TQ, :]
        o_ref[out_rows, :] = (_rms(od, sg_ref[...]) * (1.0 - lambda_init)).astype(BF16)

    @pl.when(i == 0)
    def _():
        mp_ref[...] = jnp.full(mp_ref.shape, -jnp.inf, F32)

        def body(c, carry):
            produce(c, 0)
            return carry

        lax.fori_loop(0, NKT, body, 0)
        finish_max()

    phase(1, slice(0, TQ))
    phase(2, slice(TQ, 2 * TQ))


def _attention(q, k, v, lam, subln_g, lambda_init):
    nqb = TQ // TM
    nsteps = T // (2 * TQ)
    qspec = lambda j: pl.BlockSpec((None, TM, 128),
                                   lambda b, h, i: (b, jnp.minimum(2 * nqb * i + 1 + j, NT_LAT), h))
    return pl.pallas_call(
        functools.partial(_attn_kernel, lambda_init=lambda_init),
        grid=(B, NH, nsteps),
        in_specs=[qspec(j) for j in range(3 * nqb)] + [
                  pl.BlockSpec((None, TT, 128), lambda b, h, i: (b, 0, h)),
                  pl.BlockSpec((None, TT, 128), lambda b, h, i: (b, 0, h)),
                  pl.BlockSpec((4, QK), lambda b, h, i: (0, 0)),
                  pl.BlockSpec((1, DV), lambda b, h, i: (0, 0))],
        out_specs=pl.BlockSpec((None, 2 * TQ, 128), lambda b, h, i: (b, i, h)),
        out_shape=jax.ShapeDtypeStruct((B, T, NH * DV), BF16),
        scratch_shapes=[pltpu.VMEM((3, 2 * TQ, 128), BF16),
                        pltpu.VMEM((NKT, 2 * TQ, KT), F32),
                        pltpu.VMEM((2 * TQ, 128), F32),
                        pltpu.VMEM((2 * TQ, 128), F32),
                        pltpu.VMEM((2 * TQ, 128), F32),
                        pltpu.VMEM((2 * TQ, DV), F32)],
        compiler_params=_cp(("parallel", "parallel", "arbitrary"), 58),
        name="diff_attention",
    )(*([q] * (3 * nqb)), k, v, lam, subln_g)


CONF_HALO = 16


def _conf_conv_kernel(a_ref, gt_ref, w_ref, b_ref, o_ref, vs_ref, sh_ref):
    def fill(c, carry):
        r0 = pl.multiple_of(c * TM, TM)
        v = a_ref[pl.ds(r0, TM), :].astype(F32) * _sigmoid(gt_ref[pl.ds(r0, TM), :].astype(F32))
        vs_ref[pl.ds(r0, TM), :] = jnp.where(c >= 1, v, 0.0)
        return carry

    lax.fori_loop(0, NT_CAT, fill, 0)
    vs_ref[TT:TT + CONF_HALO, :] = jnp.zeros((CONF_HALO, 128), F32)
    w = w_ref[...]
    bias = b_ref[...]

    def body(c, carry):
        start = pl.multiple_of(c * TM + (TC - CONF_HALO), 8)
        win = vs_ref[pl.ds(start, TM + 2 * CONF_HALO), :]
        acc = jnp.broadcast_to(bias, (TM, 128))
        for s in range(8):
            sh_ref[...] = win[s:s + TM + CONF_HALO + 8, :]
            for k in range(CONF_K):
                off = CONF_HALO - CONF_K // 2 + k
                if off % 8 == s:
                    acc = acc + w[k:k + 1, :] * sh_ref[off - s:off - s + TM, :]
        o_ref[pl.ds(pl.multiple_of(c * TM, TM), TM), :] = acc
        return carry

    lax.fori_loop(0, NT_LAT, body, 0)


def _conf_conv(u, w32, bias):
    nct = CONF_C // 128
    return pl.pallas_call(
        _conf_conv_kernel,
        grid=(B, nct),
        in_specs=[pl.BlockSpec((None, TT, 128), lambda b, c: (b, 0, c)),
                  pl.BlockSpec((None, TT, 128), lambda b, c: (b, 0, nct + c)),
                  pl.BlockSpec((32, 128), lambda b, c: (0, c)),
                  pl.BlockSpec((1, 128), lambda b, c: (0, c))],
        out_specs=pl.BlockSpec((None, T, 128), lambda b, c: (b, 0, c)),
        out_shape=jax.ShapeDtypeStruct((B, T, CONF_C), F32),
        scratch_shapes=[pltpu.VMEM((TT + CONF_HALO, 128), F32),
                        pltpu.VMEM((TM + CONF_HALO + 8, 128), F32)],
        compiler_params=_cp(("parallel", "parallel")),
        name="conformer_conv31",
    )(u, u, w32, bias)


def _odd_out_kernel(s_ref, oa_ref, cv_ref, cg_ref, cb_ref, wa_ref, wc_ref, g_ref, mod_ref, o_ref):
    v = cv_ref[...]
    mu = jnp.mean(v, axis=-1, keepdims=True)
    vc = v - mu
    var = jnp.mean(vc * vc, axis=-1, keepdims=True)
    cv = _silu(vc * lax.rsqrt(var + EPS) * cg_ref[...] + cb_ref[...])
    o = _dot(oa_ref[...], wa_ref[...]) + _dot(cv.astype(BF16), wc_ref[...])
    o_ref[...] = s_ref[...] + mod_ref[2:3, :] * _rms(o, g_ref[...])


def _odd_out(s_cat, oa, cvpre, cg, cb, wa, wc, g, modtab):
    full = lambda a: pl.BlockSpec(a.shape, lambda b, i: (0,) * a.ndim)
    row = lambda n: pl.BlockSpec((None, TM, n), lambda b, i: (b, i, 0))
    return pl.pallas_call(
        _odd_out_kernel,
        grid=(B, NT_LAT),
        in_specs=[pl.BlockSpec((None, TM, D), lambda b, i: (b, i + 1, 0)), row(NH * DV), row(CONF_C),
                  full(cg), full(cb), full(wa), full(wc), full(g),
                  pl.BlockSpec((None, None, 8, D), _lat_mod_idx)],
        out_specs=row(D),
        out_shape=jax.ShapeDtypeStruct((B, T, D), F32),
        compiler_params=_cp(("parallel", "parallel")),
        name="odd_out_proj",
    )(s_cat, oa, cvpre, cg, cb, wa, wc, g, modtab)


def _pad_rows(a, n):
    return jnp.pad(a, ((0, n - a.shape[0]), (0, 0)))


def _pad_lanes(a, n):
    return jnp.pad(a, ((0, 0), (0, n - a.shape[1])))


def kernel(x, c, ctx, c_ctx, mod_w, mod_b, norm_g, ffn_w_gate, ffn_w_up, ffn_w_down, ev_w_in, ev_conv_w, ev_conv_b, ev_dt_bias, ev_a_log, ev_d_skip, ev_gnorm_g, ev_w_out, od_w_in, od_lambda, od_subln_g, od_conv_w, od_conv_b, od_cnorm_g, od_cnorm_b, od_w_out):
    bf = lambda a: a.astype(BF16)

    cvec = _pad_rows(jnp.concatenate([c, c_ctx[None, :]], axis=0), 8)
    mv = _mod_vectors(cvec, mod_w, mod_b).reshape(DEPTH, 8, N_MOD, D)
    modtabs = []
    for l in range(DEPTH):
        lat = mv[l, :B]
        cx = jnp.broadcast_to(mv[l, B][None], (B, N_MOD, D))
        tab = jnp.stack([cx, lat], axis=1)
        modtabs.append(jnp.pad(tab, ((0, 0), (0, 0), (0, 8 - N_MOD), (0, 0))))

    g = norm_g[0]
    w_in = ev_w_in[0]
    o0, o1, o2 = FW, FW + SSD_INNER, FW + SSD_INNER + SSD_XBC
    f, z, xbc, dt = _even_in(ctx, x, g[0:1], modtabs[0], bf(w_in[:, :o0]), bf(w_in[:, o0:o1]),
                             bf(w_in[:, o1:o2]), bf(_pad_lanes(w_in[:, o2:], DT_PAD)))
    xbc_act = _conv5(xbc, _pad_rows(ev_conv_w[0], 8), ev_conv_b[0][None, :])
    bias_all = _pad_lanes(ev_dt_bias[0].reshape(1, -1), DT_PAD)
    alog_all = _pad_lanes(ev_a_log[0].reshape(1, -1), DT_PAD)
    yf, yb = _ssd(xbc_act, dt, bias_all, alog_all)
    fo = _fourier(f)
    dsum = jnp.repeat(ev_d_skip[0, 0] + ev_d_skip[0, 1], SSD_P)[None, :]
    w_out = ev_w_out[0]
    s = _even_out(ctx, x, fo, yf, yb, xbc_act, z, ev_gnorm_g[0][None, :], dsum, bf(w_out[:FW]), bf(w_out[FW:]),
                  g[1:2], modtabs[0])
    s = _ffn(s, g[2:3], g[3:4], modtabs[0], bf(ffn_w_gate[0]), bf(ffn_w_up[0]), bf(ffn_w_down[0]), True)

    g = norm_g[1]
    lambda_init = 0.8 - 0.6 * math.exp(-0.3 * 1)
    w_in = od_w_in[0]
    cos, sa, sb = _rope_tables()
    q, k, v, u = _odd_in(s, g[0:1], modtabs[1], bf(w_in[:, :D]), bf(w_in[:, D:2 * D]),
                         bf(w_in[:, 2 * D:3 * D]), bf(w_in[:, 3 * D:]), cos, sa, sb)
    oa = _attention(q, k, v, od_lambda[0], od_subln_g[0][None, :], lambda_init)
    cvpre = _conf_conv(u, _pad_rows(od_conv_w[0], 32), od_conv_b[0][None, :])
    w_out = od_w_out[0]
    h = _odd_out(s, oa, cvpre, od_cnorm_g[0][None, :], od_cnorm_b[0][None, :], bf(w_out[:NH * DV]),
                 bf(w_out[NH * DV:]), g[1:2], modtabs[1])
    return _ffn(h, g[2:3], g[3:4], modtabs[1], bf(ffn_w_gate[1]), bf(ffn_w_up[1]), bf(ffn_w_down[1]), False)
```

```python
import functools
import math

import numpy as np
import jax
import jax.numpy as jnp
from jax import lax
from jax.experimental import pallas as pl
from jax.experimental.pallas import tpu as pltpu

F32 = jnp.float32
BF16 = jnp.bfloat16
HIGHEST = lax.Precision.HIGHEST

D = 1024
B = 2
T = 8192
TC = 256
TT = TC + T
DEPTH = 2
N_MOD = 6
HID = 2816
EPS = 1e-6
GRID_W = 64

TM = 256
NT_CAT = TT // TM
NT_LAT = T // TM

FW = 512
FG = 4
FGD = 128
SSD_INNER = 1536
SSD_G = 4
SSD_HPG = 6
SSD_P = 64
SSD_N = 128
SSD_L = 128
SSD_XBC = 2560
SSD_K = 5
NCH_CAT = TT // SSD_L
NCH_CTX = TC // SSD_L
GW = SSD_HPG * SSD_P
DT_PAD = 128

NH = 8
QK = 64
DV = 128
ATTN_SCALE = QK ** -0.5
CONF_C = 512
CONF_K = 31
ROPE_BASE = 10000.0
TQ = 512
KT = 256
NKT = TT // KT

FT1 = 64
FT2 = 128


def _cp(sem, vmem_mb=48):
    return pltpu.CompilerParams(dimension_semantics=sem, vmem_limit_bytes=vmem_mb * 1024 * 1024)


def _sigmoid(x):
    return 1.0 / (1.0 + jnp.exp(-x))


def _silu(x):
    return x * _sigmoid(x)


def _rms(x, g):
    ms = jnp.mean(x * x, axis=-1, keepdims=True)
    return x * lax.rsqrt(ms + EPS) * g


def _dot(a, b):
    return jnp.dot(a, b, preferred_element_type=F32)


def _dot_nt(a, b):
    return lax.dot_general(a, b, (((1,), (1,)), ((), ())), preferred_element_type=F32)


def _split3(x):
    hi = x.astype(BF16)
    r1 = x - hi.astype(F32)
    mid = r1.astype(BF16)
    lo = (r1 - mid.astype(F32)).astype(BF16)
    return hi, mid, lo


def _dot3_l(m01, x):
    hi, mid, lo = _split3(x)
    return _dot(m01, hi) + _dot(m01, mid) + _dot(m01, lo)


def _dot3_r(x, m01):
    hi, mid, lo = _split3(x)
    return _dot(hi, m01) + _dot(mid, m01) + _dot(lo, m01)


def _mod_kernel(c_ref, w_ref, b_ref, o_ref):
    a = _silu(c_ref[...])
    o_ref[...] = jnp.dot(a, w_ref[...], precision=HIGHEST, preferred_element_type=F32) + b_ref[...]


def _mod_vectors(cvec, mod_w, mod_b):
    tn = 1536
    return pl.pallas_call(
        _mod_kernel,
        grid=(DEPTH, N_MOD * D // tn),
        in_specs=[
            pl.BlockSpec((8, D), lambda l, j: (0, 0)),
            pl.BlockSpec((None, D, tn), lambda l, j: (l, 0, j)),
            pl.BlockSpec((None, 1, tn), lambda l, j: (l, 0, j)),
        ],
        out_specs=pl.BlockSpec((None, 8, tn), lambda l, j: (l, 0, j)),
        out_shape=jax.ShapeDtypeStruct((DEPTH, 8, N_MOD * D), F32),
        compiler_params=_cp(("parallel", "parallel")),
        name="mod_vectors",
    )(cvec, mod_w, mod_b.reshape(DEPTH, 1, N_MOD * D))


def _cat_mod_idx(b, i):
    return (b, jnp.minimum(i, 1), 0, 0)


def _lat_mod_idx(b, i):
    return (b, 1, 0, 0)


TM_CAT = 768
TM_LAT = 512


def _row_mods(modc_ref, modl_ref, tm, cat):
    if not cat:
        return lambda r: modl_ref[r:r + 1, :]
    rows = lax.broadcasted_iota(jnp.int32, (tm, 1), 0) + pl.program_id(1) * tm
    is_ctx = rows < TC
    return lambda r: jnp.where(is_ctx, modc_ref[r:r + 1, :], modl_ref[r:r + 1, :])


def _mod_specs():
    return [pl.BlockSpec((None, None, 8, D), lambda b, i: (b, 0, 0, 0)),
            pl.BlockSpec((None, None, 8, D), lambda b, i: (b, 1, 0, 0))]


def _resident(a):
    return pl.BlockSpec(a.shape, lambda b, i: (0,) * a.ndim, pipeline_mode=pl.Buffered(1))


def _norm_mod_in(x, g_ref, mod_ref, row):
    y = _rms(x, g_ref[...])
    return y * (1.0 + mod_ref[row + 1:row + 2, :]) + mod_ref[row:row + 1, :]


NSB = TM_CAT // TM


def _stream_tile(ctx_ref, lat_refs):
    first = jnp.where(pl.program_id(1) == 0, ctx_ref[...], lat_refs[0][...])
    return jnp.concatenate([first] + [r[...] for r in lat_refs[1:]], axis=0)


def _stream_specs():
    lat = lambda j: pl.BlockSpec((None, TM, D), lambda b, i: (b, jnp.maximum(NSB * i - 1 + j, 0), 0))
    return [pl.BlockSpec((None, TM, D), lambda b, i: (b, 0, 0))] + [lat(j) for j in range(NSB)]


def _even_in_kernel(*refs):
    c_ref, l_refs = refs[0], refs[1:1 + NSB]
    g_ref, modc_ref, modl_ref, wf_ref, wz_ref, wx_ref, wd_ref, f_ref, z_ref, x_ref, dt_ref = refs[1 + NSB:]
    x = _stream_tile(c_ref, l_refs)
    mod = _row_mods(modc_ref, modl_ref, TM_CAT, True)
    a = (_rms(x, g_ref[...]) * (1.0 + mod(1)) + mod(0)).astype(BF16)
    f_ref[...] = _dot(a, wf_ref[...])
    z_ref[...] = _dot(a, wz_ref[...]).astype(BF16)
    x_ref[...] = _dot(a, wx_ref[...]).astype(BF16)
    dt_ref[...] = _dot(a, wd_ref[...])


def _even_in(ctx, x, g, modtab, wf, wz, wx, wd):
    row = lambda n: pl.BlockSpec((None, TM_CAT, n), lambda b, i: (b, i, 0))
    full = lambda a: pl.BlockSpec(a.shape, lambda b, i: (0,) * a.ndim)
    return pl.pallas_call(
        _even_in_kernel,
        grid=(B, TT // TM_CAT),
        in_specs=_stream_specs() + [full(g)] + _mod_specs() + [
            _resident(wf), _resident(wz), _resident(wx), _resident(wd)],
        out_specs=[row(FW), row(SSD_INNER), row(SSD_XBC), row(DT_PAD)],
        out_shape=[jax.ShapeDtypeStruct((B, TT, n), dt) for n, dt in
                   ((FW, F32), (SSD_INNER, BF16), (SSD_XBC, BF16), (DT_PAD, F32))],
        compiler_params=_cp(("parallel", "parallel"), 58),
        name="even_in_proj",
    )(ctx, *([x] * NSB), g, modtab, modtab, wf, wz, wx, wd)


CONV_HALO = 16


def _conv5_kernel(x_ref, w_ref, b_ref, o_ref):
    w = w_ref[...]
    bias = b_ref[...]

    def body(c, carry):
        r0 = pl.multiple_of(c * TM, TM)
        center = x_ref[pl.ds(r0, TM), :].astype(F32)
        lo = pl.multiple_of(jnp.maximum(r0 - CONV_HALO, 0), CONV_HALO)
        hi = pl.multiple_of(jnp.minimum(r0 + TM, TT - CONV_HALO), CONV_HALO)
        left = jnp.where(c >= 2, x_ref[pl.ds(lo, CONV_HALO), :].astype(F32), 0.0)
        right = jnp.where(jnp.logical_and(c >= 1, c < NT_CAT - 1),
                          x_ref[pl.ds(hi, CONV_HALO), :].astype(F32), 0.0)
        ext = jnp.concatenate([left, center, right], axis=0)
        acc = bias + w[2:3, :] * center
        for k in (0, 1, 3, 4):
            off = CONV_HALO - SSD_K // 2 + k
            acc = acc + w[k:k + 1, :] * ext[off:off + TM, :]
        o_ref[pl.ds(r0, TM), :] = _silu(acc).astype(BF16)
        return carry

    lax.fori_loop(0, NT_CAT, body, 0)


def _conv5(xbc, w8, bias):
    return pl.pallas_call(
        _conv5_kernel,
        grid=(B, SSD_XBC // 128),
        in_specs=[
            pl.BlockSpec((None, TT, 128), lambda b, c: (b, 0, c)),
            pl.BlockSpec((8, 128), lambda b, c: (0, c)),
            pl.BlockSpec((1, 128), lambda b, c: (0, c)),
        ],
        out_specs=pl.BlockSpec((None, TT, 128), lambda b, c: (b, 0, c)),
        out_shape=jax.ShapeDtypeStruct((B, TT, SSD_XBC), BF16),
        compiler_params=_cp(("parallel", "parallel")),
        name="ssd_conv5",
    )(xbc, w8, bias)


def _ssd_bwd_chunk(i):
    return jnp.where(i < NCH_CTX, NCH_CTX - 1 - i, NCH_CAT + NCH_CTX - 1 - i)


SSD_NH = SSD_G * SSD_HPG


def _ssd_expand_table():
    lane = np.arange(DT_PAD)[:, None]
    col = np.arange(SSD_INNER)[None, :]
    tabs = []
    for d in range(2):
        e = (lane == d * SSD_NH + col // SSD_P).astype(np.float32)
        tabs.append(np.concatenate([e, e], axis=0))
    return jnp.asarray(np.stack(tabs), dtype=BF16)


def _hi_mid(x):
    hi = x.astype(BF16)
    mid = (x - hi.astype(F32)).astype(BF16)
    return jnp.concatenate([hi, mid], axis=1)


def _ssd_kernel(xf_ref, bf_ref, cf_ref, dtf_ref, xb_ref, bb_ref, cb_ref, dtb_ref, bias_ref, alog_ref,
                ep_ref, yf_ref, yb_ref, st_ref):
    @pl.when(pl.program_id(1) == 0)
    def _():
        st_ref[...] = jnp.zeros_like(st_ref)

    row = lax.broadcasted_iota(jnp.int32, (SSD_L, SSD_L), 0)
    col = lax.broadcasted_iota(jnp.int32, (SSD_L, SSD_L), 1)
    lane_lo = lax.broadcasted_iota(jnp.int32, (SSD_L, 128), 1) < SSD_P
    bias = bias_ref[...]
    a_all = -jnp.exp(alog_ref[...]) * math.log2(math.e)
    dirs = ((xf_ref, bf_ref, cf_ref, dtf_ref, yf_ref), (xb_ref, bb_ref, cb_ref, dtb_ref, yb_ref))

    for d, (xs_ref, bm_ref, cm_ref, dt_ref, y_ref) in enumerate(dirs):
        tri = (row >= col) if d == 0 else (row <= col)
        tri_b = jnp.where(tri, 1.0, 0.0).astype(BF16)
        raw = dt_ref[...] + bias
        dt_all = jnp.maximum(raw, 0.0) + jnp.log1p(jnp.exp(-jnp.abs(raw)))
        acs = _dot3_l(tri_b, dt_all * a_all)
        acs_t = acs.T
        dt_e = _dot(_hi_mid(dt_all), ep_ref[d])
        acs_e = _dot(_hi_mid(acs), ep_ref[d])
        last = SSD_L - 1 if d == 0 else 0
        tot_e = acs_e[last:last + 1, :]

        def head_mat(cb, h):
            c = d * SSD_NH + h
            seg = acs[:, c:c + 1] - acs_t[c:c + 1, :]
            dec = jnp.exp2(jnp.where(tri, seg, -jnp.inf))
            return (cb * dec).astype(BF16)

        for g in range(SSD_G):
            cm_b = cm_ref[:, g * SSD_N:(g + 1) * SSD_N]
            bm = bm_ref[:, g * SSD_N:(g + 1) * SSD_N]
            cb = _dot_nt(cm_b, bm)
            bm_t = bm.astype(F32).T.astype(BF16)
            glanes = slice(g * GW, (g + 1) * GW)
            acs_g = acs_e[:, glanes]
            tot_g = tot_e[:, glanes]
            xdt = xs_ref[:, glanes].astype(F32) * dt_e[:, glanes]
            st_old = st_ref[d, g]
            y_off = _dot(cm_b, st_old.astype(BF16)) * jnp.exp2(acs_g)
            st_ref[d, g] = st_old * jnp.exp2(tot_g) + _dot(bm_t, (xdt * jnp.exp2(tot_g - acs_g)).astype(BF16))
            for j in range(SSD_HPG // 2):
                h0 = g * SSD_HPG + 2 * j
                x_p = xdt[:, j * 128:(j + 1) * 128]
                x2 = jnp.concatenate([jnp.where(lane_lo, x_p, 0.0), jnp.where(lane_lo, 0.0, x_p)], axis=0)
                m2 = jnp.concatenate([head_mat(cb, h0), head_mat(cb, h0 + 1)], axis=1)
                y_diag = _dot(m2, x2.astype(BF16))
                lanes = slice(g * GW + j * 128, g * GW + (j + 1) * 128)
                y_ref[:, lanes] = (y_diag + y_off[:, j * 128:(j + 1) * 128]).astype(BF16)


def _ssd(xbc_act, dt, bias_all, alog_all):
    nbc = SSD_INNER // (SSD_G * SSD_N)
    ep = _ssd_expand_table()
    fwd = lambda blk: (lambda b, i: (b, i, blk))
    bwd = lambda blk: (lambda b, i: (b, _ssd_bwd_chunk(i), blk))
    specs = lambda ix: [pl.BlockSpec((None, SSD_L, SSD_INNER), ix(0)),
                        pl.BlockSpec((None, SSD_L, SSD_G * SSD_N), ix(nbc)),
                        pl.BlockSpec((None, SSD_L, SSD_G * SSD_N), ix(nbc + 1)),
                        pl.BlockSpec((None, SSD_L, DT_PAD), ix(0))]
    par = pl.BlockSpec((1, DT_PAD), lambda b, i: (0, 0))
    full = lambda a: pl.BlockSpec(a.shape, lambda b, i: (0,) * a.ndim)
    return pl.pallas_call(
        _ssd_kernel,
        grid=(B, NCH_CAT),
        in_specs=specs(fwd) + specs(bwd) + [par, par, full(ep)],
        out_specs=[pl.BlockSpec((None, SSD_L, SSD_INNER), fwd(0)),
                   pl.BlockSpec((None, SSD_L, SSD_INNER), bwd(0))],
        out_shape=[jax.ShapeDtypeStruct((B, TT, SSD_INNER), BF16)] * 2,
        scratch_shapes=[pltpu.VMEM((2, SSD_G, SSD_N, GW), F32)],
        compiler_params=_cp(("parallel", "arbitrary")),
        name="ssd_scan",
    )(xbc_act, xbc_act, xbc_act, dt, xbc_act, xbc_act, xbc_act, dt, bias_all, alog_all, ep)


def _dft_cs(n):
    k = np.arange(n, dtype=np.int64)
    ang = 2.0 * np.pi * ((k[:, None] * k[None, :]) % n).astype(np.float64) / n
    return np.cos(ang), np.sin(ang)


def _hi_lo(a):
    a32 = jnp.asarray(a, dtype=F32)
    hi = a32.astype(BF16)
    lo = (a32 - hi.astype(F32)).astype(BF16)
    return jnp.stack([hi, lo], axis=0)


def _fourier_tables():
    c1, s1 = _dft_cs(FT1)
    w1 = np.concatenate([c1, -s1], axis=0)
    r = np.arange(FT2)
    t2_of_r = F_OCT * (r % (FT2 // F_OCT)) + r // (FT2 // F_OCT)
    c2, s2 = _dft_cs(FT2)
    c2, s2 = c2[:, t2_of_r], s2[:, t2_of_r]
    w2 = np.block([[c2, s2], [-s2, c2]])
    cc, sc = _dft_cs(FGD)
    wc = np.concatenate([cc, sc], axis=0) / math.sqrt(T * FGD)
    ang = 2.0 * np.pi * t2_of_r.astype(np.float64)[:, None] / T
    tw = np.stack([np.cos(ang), np.sin(ang)], axis=0) * np.ones((1, 1, FGD))
    cx, sx = _dft_cs(TC)
    wx = np.concatenate([cx, sx], axis=0)
    wcx = np.concatenate([cc, -sc], axis=0) / math.sqrt(TC * FGD)
    return _hi_lo(w1), _hi_lo(w2), _hi_lo(wc), jnp.asarray(tw, dtype=F32), _hi_lo(wx), _hi_lo(wcx)


def _dot_hl(a_hl, x):
    xh = x.astype(BF16)
    xl = (x - xh.astype(F32)).astype(BF16)
    return _dot(a_hl[0], xh) + _dot(a_hl[0], xl) + _dot(a_hl[1], xh)


def _dot_lh(x, b_hl):
    xh = x.astype(BF16)
    xl = (x - xh.astype(F32)).astype(BF16)
    return _dot(xh, b_hl[0]) + _dot(xl, b_hl[0]) + _dot(xh, b_hl[1])


F_OCT = 8


def _fourier_kernel(x_ref, w1_ref, w2_ref, wc_ref, tw_ref, wx_ref, wcx_ref, o_ref, y1_ref):
    w1 = (w1_ref[0], w1_ref[1])
    w2 = (w2_ref[0], w2_ref[1])
    wc = (wc_ref[0], wc_ref[1])

    noct = FT2 // F_OCT
    for j in range(noct):
        cols = [x_ref[pl.ds(TC + F_OCT * j + q, FT1, stride=FT2), :] for q in range(F_OCT)]
        y = _dot_hl(w1, jnp.concatenate(cols, axis=1))
        for q in range(F_OCT):
            y1_ref[q, j * 2 * FT1:(j + 1) * 2 * FT1, :] = y[:, q * FGD:(q + 1) * FGD]

    def gather(kk):
        return jnp.concatenate([y1_ref[q, pl.ds(kk, noct, stride=2 * FT1), :] for q in range(F_OCT)], axis=0)

    cstep = tw_ref[0]
    sstep = tw_ref[1]

    def rotate(c, s):
        return c * cstep - s * sstep, s * cstep + c * sstep

    def twiddled(k1, c, s):
        yr = gather(k1)
        yi = gather(FT1 + k1)
        return yr * c + yi * s, yi * c - yr * s

    def body(i, carry):
        c0, s0 = carry
        c1, s1 = rotate(c0, s0)
        k1 = 2 * i
        re0, im0 = twiddled(k1, c0, s0)
        re1, im1 = twiddled(k1 + 1, c1, s1)
        y2 = jnp.concatenate([jnp.concatenate([re0, re1], axis=1),
                              jnp.concatenate([im0, im1], axis=1)], axis=0)
        y3 = _dot_hl(w2, y2)
        lhs = jnp.concatenate([jnp.concatenate([y3[:FT2, :FGD], y3[FT2:, :FGD]], axis=1),
                               jnp.concatenate([y3[:FT2, FGD:], y3[FT2:, FGD:]], axis=1)], axis=0)
        res = _dot_lh(lhs, wc)
        o_ref[pl.ds(TC + k1, FT2, stride=FT1), :] = res[:FT2, :]
        o_ref[pl.ds(TC + k1 + 1, FT2, stride=FT1), :] = res[FT2:, :]
        return rotate(c1, s1)

    init = (jnp.ones((FT2, FGD), F32), jnp.zeros((FT2, FGD), F32))
    lax.fori_loop(0, FT1 // 2, body, init, unroll=8)

    ab = _dot_hl((wx_ref[0], wx_ref[1]), x_ref[0:TC, :])
    lhs = jnp.concatenate([ab[:TC, :], ab[TC:, :]], axis=1)
    o_ref[0:TC, :] = _dot_lh(lhs, (wcx_ref[0], wcx_ref[1]))


def _fourier(f_cat):
    w1, w2, wc, tw, wx, wcx = _fourier_tables()
    full = lambda a: pl.BlockSpec(a.shape, lambda b, g: (0,) * a.ndim)
    blk = pl.BlockSpec((None, TT, FGD), lambda b, g: (b, 0, g))
    return pl.pallas_call(
        _fourier_kernel,
        grid=(B, FG),
        in_specs=[blk, full(w1), full(w2), full(wc), full(tw), full(wx), full(wcx)],
        out_specs=blk,
        out_shape=jax.ShapeDtypeStruct((B, TT, FW), F32),
        scratch_shapes=[pltpu.VMEM((F_OCT, FT2 // F_OCT * 2 * FT1, FGD), F32)],
        compiler_params=_cp(("parallel", "parallel")),
        name="fourier_mix",
    )(f_cat, w1, w2, wc, tw, wx, wcx)


def _even_out_kernel(*refs):
    c_ref, l_refs = refs[0], refs[1:1 + NSB]
    (fo_ref, yf_ref, yb_ref, xs_ref, z_ref, gn_ref, ds_ref, wf_ref, wy_ref, g_ref, modc_ref, modl_ref,
     o_ref) = refs[1 + NSB:]
    y = yf_ref[...].astype(F32) + yb_ref[...].astype(F32) + ds_ref[...] * xs_ref[...].astype(F32)
    gated = y * _silu(z_ref[...].astype(F32))
    gn = gn_ref[...]
    parts = []
    for g in range(SSD_G):
        parts.append(_rms(gated[:, g * GW:(g + 1) * GW], gn[:, g * GW:(g + 1) * GW]))
    yn = jnp.concatenate(parts, axis=1).astype(BF16)
    o = _dot(fo_ref[...].astype(BF16), wf_ref[...]) + _dot(yn, wy_ref[...])
    gate = _row_mods(modc_ref, modl_ref, TM_CAT, True)(2)
    o_ref[...] = _stream_tile(c_ref, l_refs) + gate * _rms(o, g_ref[...])


def _even_out(ctx, x, fo, yf, yb, xbc_act, z, gn, dsum, wf, wy, g, modtab):
    row = lambda n: pl.BlockSpec((None, TM_CAT, n), lambda b, i: (b, i, 0))
    full = lambda a: pl.BlockSpec(a.shape, lambda b, i: (0,) * a.ndim)
    return pl.pallas_call(
        _even_out_kernel,
        grid=(B, TT // TM_CAT),
        in_specs=_stream_specs() + [row(FW), row(SSD_INNER), row(SSD_INNER),
                                    row(SSD_INNER), row(SSD_INNER), full(gn), full(dsum),
                                    _resident(wf), _resident(wy), full(g)] + _mod_specs(),
        out_specs=row(D),
        out_shape=jax.ShapeDtypeStruct((B, TT, D), F32),
        compiler_params=_cp(("parallel", "parallel"), 58),
        name="even_out_proj",
    )(ctx, *([x] * NSB), fo, yf, yb, xbc_act, z, gn, dsum, wf, wy, g, modtab, modtab)


def _ffn_kernel(s_ref, g2_ref, g3_ref, modc_ref, modl_ref, wg_ref, wu_ref, wd_ref, o_ref, *, cat):
    x = s_ref[...]
    mod = _row_mods(modc_ref, modl_ref, x.shape[0], cat)
    a = (_rms(x, g2_ref[...]) * (1.0 + mod(4)) + mod(3)).astype(BF16)
    gt = _dot(a, wg_ref[...])
    up = _dot(a, wu_ref[...])
    hid = (_silu(gt) * up).astype(BF16)
    f = _dot(hid, wd_ref[...])
    o_ref[...] = x + mod(5) * _rms(f, g3_ref[...])


def _ffn(s, g2, g3, modtab, wg, wu, wd, cat):
    tm = TM_CAT if cat else TM_LAT
    rows = TT if cat else T
    row = pl.BlockSpec((None, tm, D), lambda b, i: (b, i, 0))
    full = lambda a: pl.BlockSpec(a.shape, lambda b, i: (0,) * a.ndim)
    return pl.pallas_call(
        functools.partial(_ffn_kernel, cat=cat),
        grid=(B, rows // tm),
        in_specs=[row, full(g2), full(g3)] + _mod_specs() + [_resident(wg), _resident(wu), _resident(wd)],
        out_specs=row,
        out_shape=jax.ShapeDtypeStruct((B, rows, D), F32),
        compiler_params=_cp(("parallel", "parallel"), 60),
        name="ffn_cat" if cat else "ffn_lat",
    )(s, g2, g3, modtab, modtab, wg, wu, wd)


def _rope_tables():
    rows = T // GRID_W
    row = jnp.repeat(jnp.arange(rows, dtype=F32), GRID_W)
    col = jnp.tile(jnp.arange(GRID_W, dtype=F32), rows)
    nf = QK // 4
    inv_freq = ROPE_BASE ** (-jnp.arange(nf, dtype=F32) * 2.0 / (QK // 2))
    ang = jnp.stack([row, col], axis=-1)[:, :, None] * inv_freq
    cos, sin = jnp.cos(ang), jnp.sin(ang)
    lane = np.arange(128)
    axis = (lane % QK) // (QK // 2)
    half = (lane % (QK // 2)) // nf
    freq = lane % nf
    cos_l = cos[:, axis, freq]
    sin_l = sin[:, axis, freq]
    sa = jnp.where(jnp.asarray(half == 0)[None, :], -sin_l, 0.0)
    sb = jnp.where(jnp.asarray(half == 1)[None, :], sin_l, 0.0)
    ctx1 = jnp.ones((TC, 128), F32)
    ctx0 = jnp.zeros((TC, 128), F32)
    return (jnp.concatenate([ctx1, cos_l], axis=0), jnp.concatenate([ctx0, sa], axis=0),
            jnp.concatenate([ctx0, sb], axis=0))


def _odd_in_kernel(s_ref, g_ref, modc_ref, modl_ref, wq_ref, wk_ref, wv_ref, wu_ref, cos_ref, sa_ref, sb_ref,
                   q_ref, k_ref, v_ref, u_ref):
    x = s_ref[...]
    mod = _row_mods(modc_ref, modl_ref, x.shape[0], True)
    a = (_rms(x, g_ref[...]) * (1.0 + mod(1)) + mod(0)).astype(BF16)
    cos = cos_ref[...]
    sa = sa_ref[...]
    sb = sb_ref[...]

    def rope(x, scale):
        outs = []
        for h in range(NH):
            xh = x[:, h * 128:(h + 1) * 128]
            r = xh * cos + pltpu.roll(xh, 112, 1) * sa + pltpu.roll(xh, 16, 1) * sb
            outs.append(r * scale if scale != 1.0 else r)
        return jnp.concatenate(outs, axis=1)

    q_ref[...] = rope(_dot(a, wq_ref[...]), ATTN_SCALE * math.log2(math.e)).astype(BF16)
    k_ref[...] = rope(_dot(a, wk_ref[...]), 1.0).astype(BF16)
    v_ref[...] = _dot(a, wv_ref[...]).astype(BF16)
    u_ref[...] = _dot(a, wu_ref[...]).astype(BF16)


def _odd_in(s, g, modtab, wq, wk, wv, wu, cos, sa, sb):
    row = lambda n: pl.BlockSpec((None, TM_CAT, n), lambda b, i: (b, i, 0))
    full = lambda a: pl.BlockSpec(a.shape, lambda b, i: (0,) * a.ndim)
    tab = pl.BlockSpec((TM_CAT, 128), lambda b, i: (i, 0))
    return pl.pallas_call(
        _odd_in_kernel,
        grid=(B, TT // TM_CAT),
        in_specs=[row(D), full(g)] + _mod_specs() + [
                  _resident(wq), _resident(wk), _resident(wv), _resident(wu), tab, tab, tab],
        out_specs=[row(D), row(D), row(D), row(D)],
        out_shape=[jax.ShapeDtypeStruct((B, TT, D), BF16)] * 4,
        compiler_params=_cp(("parallel", "parallel"), 56),
        name="odd_in_proj",
    )(s, g, modtab, modtab, wq, wk, wv, wu, cos, sa, sb)


def _attn_kernel(*refs, lambda_init):
    nqb = TQ // TM
    q_refs = refs[:3 * nqb]
    k_ref, v_ref, lam_ref, sg_ref, o_ref, q2_ref, s_ref, mp_ref, mb_ref, ls_ref, acc_ref = refs[3 * nqb:]
    i = pl.program_id(2)
    lane_lo = lax.broadcasted_iota(jnp.int32, (TM, 128), 1) < QK

    for slot in range(3):
        for j in range(nqb):
            q = q_refs[slot * nqb + j][...]
            zero = jnp.zeros_like(q)
            q2_ref[slot, j * TM:(j + 1) * TM, :] = jnp.where(lane_lo, q, zero)
            q2_ref[slot, TQ + j * TM:TQ + (j + 1) * TM, :] = jnp.where(lane_lo, zero, q)

    def produce(c, qslot):
        r0 = pl.multiple_of(c * KT, KT)
        s = _dot_nt(q2_ref[qslot], k_ref[pl.ds(r0, KT), :])
        s_ref[c] = s
        mp_ref[...] = jnp.maximum(mp_ref[...], jnp.maximum(s[:, 0:128], s[:, 128:256]))

    def finish_max():
        m = jnp.max(mp_ref[...], axis=-1, keepdims=True)
        mb_ref[...] = jnp.broadcast_to(m, mb_ref.shape)
        mp_ref[...] = jnp.full(mp_ref.shape, -jnp.inf, F32)

    def consume(c):
        r0 = pl.multiple_of(c * KT, KT)
        s = s_ref[c]
        mb = mb_ref[...]
        p0 = jnp.exp2(s[:, 0:128] - mb)
        p1 = jnp.exp2(s[:, 128:256] - mb)
        ls_ref[...] += p0 + p1
        p = jnp.concatenate([p0, p1], axis=1).astype(BF16)
        acc_ref[...] += _dot(p, v_ref[pl.ds(r0, KT), :])

    lf = lam_ref[...]
    lam = (jnp.exp(jnp.sum(lf[0:1, :] * lf[1:2, :], axis=-1, keepdims=True))
           - jnp.exp(jnp.sum(lf[2:3, :] * lf[3:4, :], axis=-1, keepdims=True)) + lambda_init)

    def phase(next_q, out_rows):
        ls_ref[...] = jnp.zeros(ls_ref.shape, F32)
        acc_ref[...] = jnp.zeros(acc_ref.shape, F32)
        def body(c, carry):
            consume(c)
            produce(c, next_q)
            return carry

        lax.fori_loop(0, NKT, body, 0, unroll=11)
        finish_max()
        o = acc_ref[...] / jnp.sum(ls_ref[...], axis=-1, keepdims=True)
        od = o[0:TQ, :] - lam * o[TQ:2 * TQ, :]
        o_ref[out_rows, :] = (_rms(od, sg_ref[...]) * (1.0 - lambda_init)).astype(BF16)

    @pl.when(i == 0)
    def _():
        mp_ref[...] = jnp.full(mp_ref.shape, -jnp.inf, F32)

        def body(c, carry):
            produce(c, 0)
            return carry

        lax.fori_loop(0, NKT, body, 0)
        finish_max()

    phase(1, slice(0, TQ))
    phase(2, slice(TQ, 2 * TQ))


def _attention(q, k, v, lam, subln_g, lambda_init):
    nqb = TQ // TM
    nsteps = T // (2 * TQ)
    qspec = lambda j: pl.BlockSpec((None, TM, 128),
                                   lambda b, h, i: (b, jnp.minimum(2 * nqb * i + 1 + j, NT_LAT), h))
    return pl.pallas_call(
        functools.partial(_attn_kernel, lambda_init=lambda_init),
        grid=(B, NH, nsteps),
        in_specs=[qspec(j) for j in range(3 * nqb)] + [
                  pl.BlockSpec((None, TT, 128), lambda b, h, i: (b, 0, h)),
                  pl.BlockSpec((None, TT, 128), lambda b, h, i: (b, 0, h)),
                  pl.BlockSpec((4, QK), lambda b, h, i: (0, 0)),
                  pl.BlockSpec((1, DV), lambda b, h, i: (0, 0))],
        out_specs=pl.BlockSpec((None, 2 * TQ, 128), lambda b, h, i: (b, i, h)),
        out_shape=jax.ShapeDtypeStruct((B, T, NH * DV), BF16),
        scratch_shapes=[pltpu.VMEM((3, 2 * TQ, 128), BF16),
                        pltpu.VMEM((NKT, 2 * TQ, KT), F32),
                        pltpu.VMEM((2 * TQ, 128), F32),
                        pltpu.VMEM((2 * TQ, 128), F32),
                        pltpu.VMEM((2 * TQ, 128), F32),
                        pltpu.VMEM((2 * TQ, DV), F32)],
        compiler_params=_cp(("parallel", "parallel", "arbitrary"), 58),
        name="diff_attention",
    )(*([q] * (3 * nqb)), k, v, lam, subln_g)


CONF_HALO = 16


def _conf_conv_kernel(a_ref, gt_ref, w_ref, b_ref, o_ref, vs_ref, sh_ref):
    def fill(c, carry):
        r0 = pl.multiple_of(c * TM, TM)
        v = a_ref[pl.ds(r0, TM), :].astype(F32) * _sigmoid(gt_ref[pl.ds(r0, TM), :].astype(F32))
        vs_ref[pl.ds(r0, TM), :] = jnp.where(c >= 1, v, 0.0)
        return carry

    lax.fori_loop(0, NT_CAT, fill, 0)
    vs_ref[TT:TT + CONF_HALO, :] = jnp.zeros((CONF_HALO, 128), F32)
    w = w_ref[...]
    bias = b_ref[...]

    def body(c, carry):
        start = pl.multiple_of(c * TM + (TC - CONF_HALO), 8)
        win = vs_ref[pl.ds(start, TM + 2 * CONF_HALO), :]
        acc = jnp.broadcast_to(bias, (TM, 128))
        for s in range(8):
            sh_ref[...] = win[s:s + TM + CONF_HALO + 8, :]
            for k in range(CONF_K):
                off = CONF_HALO - CONF_K // 2 + k
                if off % 8 == s:
                    acc = acc + w[k:k + 1, :] * sh_ref[off - s:off - s + TM, :]
        o_ref[pl.ds(pl.multiple_of(c * TM, TM), TM), :] = acc
        return carry

    lax.fori_loop(0, NT_LAT, body, 0)


def _conf_conv(u, w32, bias):
    nct = CONF_C // 128
    return pl.pallas_call(
        _conf_conv_kernel,
        grid=(B, nct),
        in_specs=[pl.BlockSpec((None, TT, 128), lambda b, c: (b, 0, c)),
                  pl.BlockSpec((None, TT, 128), lambda b, c: (b, 0, nct + c)),
                  pl.BlockSpec((32, 128), lambda b, c: (0, c)),
                  pl.BlockSpec((1, 128), lambda b, c: (0, c))],
        out_specs=pl.BlockSpec((None, T, 128), lambda b, c: (b, 0, c)),
        out_shape=jax.ShapeDtypeStruct((B, T, CONF_C), F32),
        scratch_shapes=[pltpu.VMEM((TT + CONF_HALO, 128), F32),
                        pltpu.VMEM((TM + CONF_HALO + 8, 128), F32)],
        compiler_params=_cp(("parallel", "parallel")),
        name="conformer_conv31",
    )(u, u, w32, bias)


def _odd_out_kernel(s0_ref, s1_ref, oa_ref, cv_ref, cg_ref, cb_ref, wa_ref, wc_ref, g_ref, mod_ref, o_ref):
    v = cv_ref[...]
    mu = jnp.mean(v, axis=-1, keepdims=True)
    vc = v - mu
    var = jnp.mean(vc * vc, axis=-1, keepdims=True)
    cv = _silu(vc * lax.rsqrt(var + EPS) * cg_ref[...] + cb_ref[...])
    o = _dot(oa_ref[...], wa_ref[...]) + _dot(cv.astype(BF16), wc_ref[...])
    resid = jnp.concatenate([s0_ref[...], s1_ref[...]], axis=0)
    o_ref[...] = resid + mod_ref[2:3, :] * _rms(o, g_ref[...])


def _odd_out(s_cat, oa, cvpre, cg, cb, wa, wc, g, modtab):
    full = lambda a: pl.BlockSpec(a.shape, lambda b, i: (0,) * a.ndim)
    row = lambda n: pl.BlockSpec((None, TM_LAT, n), lambda b, i: (b, i, 0))
    assert TM_LAT == 2 * TM
    return pl.pallas_call(
        _odd_out_kernel,
        grid=(B, T // TM_LAT),
        in_specs=[pl.BlockSpec((None, TM, D), lambda b, i: (b, 2 * i + 1, 0)),
                  pl.BlockSpec((None, TM, D), lambda b, i: (b, 2 * i + 2, 0)),
                  row(NH * DV), row(CONF_C),
                  full(cg), full(cb), full(wa), full(wc), full(g),
                  pl.BlockSpec((None, None, 8, D), _lat_mod_idx)],
        out_specs=row(D),
        out_shape=jax.ShapeDtypeStruct((B, T, D), F32),
        compiler_params=_cp(("parallel", "parallel")),
        name="odd_out_proj",
    )(s_cat, s_cat, oa, cvpre, cg, cb, wa, wc, g, modtab)


def _pad_rows(a, n):
    return jnp.pad(a, ((0, n - a.shape[0]), (0, 0)))


def _pad_lanes(a, n):
    return jnp.pad(a, ((0, 0), (0, n - a.shape[1])))


def kernel(x, c, ctx, c_ctx, mod_w, mod_b, norm_g, ffn_w_gate, ffn_w_up, ffn_w_down, ev_w_in, ev_conv_w, ev_conv_b, ev_dt_bias, ev_a_log, ev_d_skip, ev_gnorm_g, ev_w_out, od_w_in, od_lambda, od_subln_g, od_conv_w, od_conv_b, od_cnorm_g, od_cnorm_b, od_w_out):
    bf = lambda a: a.astype(BF16)

    cvec = _pad_rows(jnp.concatenate([c, c_ctx[None, :]], axis=0), 8)
    mv = _mod_vectors(cvec, mod_w, mod_b).reshape(DEPTH, 8, N_MOD, D)
    modtabs = []
    for l in range(DEPTH):
        lat = mv[l, :B]
        cx = jnp.broadcast_to(mv[l, B][None], (B, N_MOD, D))
        tab = jnp.stack([cx, lat], axis=1)
        modtabs.append(jnp.pad(tab, ((0, 0), (0, 0), (0, 8 - N_MOD), (0, 0))))

    g = norm_g[0]
    w_in = ev_w_in[0]
    o0, o1, o2 = FW, FW + SSD_INNER, FW + SSD_INNER + SSD_XBC
    f, z, xbc, dt = _even_in(ctx, x, g[0:1], modtabs[0], bf(w_in[:, :o0]), bf(w_in[:, o0:o1]),
                             bf(w_in[:, o1:o2]), bf(_pad_lanes(w_in[:, o2:], DT_PAD)))
    xbc_act = _conv5(xbc, _pad_rows(ev_conv_w[0], 8), ev_conv_b[0][None, :])
    bias_all = _pad_lanes(ev_dt_bias[0].reshape(1, -1), DT_PAD)
    alog_all = _pad_lanes(ev_a_log[0].reshape(1, -1), DT_PAD)
    yf, yb = _ssd(xbc_act, dt, bias_all, alog_all)
    fo = _fourier(f)
    dsum = jnp.repeat(ev_d_skip[0, 0] + ev_d_skip[0, 1], SSD_P)[None, :]
    w_out = ev_w_out[0]
    s = _even_out(ctx, x, fo, yf, yb, xbc_act, z, ev_gnorm_g[0][None, :], dsum, bf(w_out[:FW]), bf(w_out[FW:]),
                  g[1:2], modtabs[0])
    s = _ffn(s, g[2:3], g[3:4], modtabs[0], bf(ffn_w_gate[0]), bf(ffn_w_up[0]), bf(ffn_w_down[0]), True)

    g = norm_g[1]
    lambda_init = 0.8 - 0.6 * math.exp(-0.3 * 1)
    w_in = od_w_in[0]
    cos, sa, sb = _rope_tables()
    q, k, v, u = _odd_in(s, g[0:1], modtabs[1], bf(w_in[:, :D]), bf(w_in[:, D:2 * D]),
                         bf(w_in[:, 2 * D:3 * D]), bf(w_in[:, 3 * D:]), cos, sa, sb)
    oa = _attention(q, k, v, od_lambda[0], od_subln_g[0][None, :], lambda_init)
    cvpre = _conf_conv(u, _pad_rows(od_conv_w[0], 32), od_conv_b[0][None, :])
    w_out = od_w_out[0]
    h = _odd_out(s, oa, cvpre, od_cnorm_g[0][None, :], od_cnorm_b[0][None, :], bf(w_out[:NH * DV]),
                 bf(w_out[NH * DV:]), g[1:2], modtabs[1])
    return _ffn(h, g[2:3], g[3:4], modtabs[1], bf(ffn_w_gate[1]), bf(ffn_w_up[1]), bf(ffn_w_down[1]), False)
```

```python
import functools
import math

import numpy as np
import jax
import jax.numpy as jnp
from jax import lax
from jax.experimental import pallas as pl
from jax.experimental.pallas import tpu as pltpu

F32 = jnp.float32
BF16 = jnp.bfloat16
HIGHEST = lax.Precision.HIGHEST

D = 1024
B = 2
T = 8192
TC = 256
TT = TC + T
DEPTH = 2
N_MOD = 6
HID = 2816
EPS = 1e-6
GRID_W = 64

TM = 256
NT_CAT = TT // TM
NT_LAT = T // TM

FW = 512
FG = 4
FGD = 128
SSD_INNER = 1536
SSD_G = 4
SSD_HPG = 6
SSD_P = 64
SSD_N = 128
SSD_L = 128
SSD_XBC = 2560
SSD_K = 5
NCH_CAT = TT // SSD_L
NCH_CTX = TC // SSD_L
GW = SSD_HPG * SSD_P
DT_PAD = 128

NH = 8
QK = 64
DV = 128
ATTN_SCALE = QK ** -0.5
CONF_C = 512
CONF_K = 31
ROPE_BASE = 10000.0
TQ = 512
KT = 256
NKT = TT // KT
ATT_UNROLL = 11

FT1 = 64
FT2 = 128


def _cp(sem, vmem_mb=48):
    return pltpu.CompilerParams(dimension_semantics=sem, vmem_limit_bytes=vmem_mb * 1024 * 1024)


def _sigmoid(x):
    return 1.0 / (1.0 + jnp.exp(-x))


def _silu(x):
    return x * _sigmoid(x)


def _rms(x, g):
    ms = jnp.mean(x * x, axis=-1, keepdims=True)
    return x * lax.rsqrt(ms + EPS) * g


def _dot(a, b):
    return jnp.dot(a, b, preferred_element_type=F32)


def _dot_nt(a, b):
    return lax.dot_general(a, b, (((1,), (1,)), ((), ())), preferred_element_type=F32)


def _split3(x):
    hi = x.astype(BF16)
    r1 = x - hi.astype(F32)
    mid = r1.astype(BF16)
    lo = (r1 - mid.astype(F32)).astype(BF16)
    return hi, mid, lo


def _dot3_l(m01, x):
    hi, mid, lo = _split3(x)
    return _dot(m01, hi) + _dot(m01, mid) + _dot(m01, lo)


def _dot3_r(x, m01):
    hi, mid, lo = _split3(x)
    return _dot(hi, m01) + _dot(mid, m01) + _dot(lo, m01)


def _mod_kernel(c_ref, w_ref, b_ref, o_ref):
    a = _silu(c_ref[...])
    o_ref[...] = jnp.dot(a, w_ref[...], precision=HIGHEST, preferred_element_type=F32) + b_ref[...]


def _mod_vectors(cvec, mod_w, mod_b):
    tn = 1536
    return pl.pallas_call(
        _mod_kernel,
        grid=(DEPTH, N_MOD * D // tn),
        in_specs=[
            pl.BlockSpec((8, D), lambda l, j: (0, 0)),
            pl.BlockSpec((None, D, tn), lambda l, j: (l, 0, j)),
            pl.BlockSpec((None, 1, tn), lambda l, j: (l, 0, j)),
        ],
        out_specs=pl.BlockSpec((None, 8, tn), lambda l, j: (l, 0, j)),
        out_shape=jax.ShapeDtypeStruct((DEPTH, 8, N_MOD * D), F32),
        compiler_params=_cp(("parallel", "parallel")),
        name="mod_vectors",
    )(cvec, mod_w, mod_b.reshape(DEPTH, 1, N_MOD * D))


def _lat_mod_idx(b, i):
    return (b, 1, 0, 0)


TM_CAT = 768
TM_LAT = 512
TM_FFN_CAT = 256


def _row_mods(modc_ref, modl_ref, tm, cat):
    if not cat:
        return lambda r: modl_ref[r:r + 1, :]
    rows = lax.broadcasted_iota(jnp.int32, (tm, 1), 0) + pl.program_id(1) * tm
    is_ctx = rows < TC
    return lambda r: jnp.where(is_ctx, modc_ref[r:r + 1, :], modl_ref[r:r + 1, :])


def _mod_specs():
    return [pl.BlockSpec((None, None, 8, D), lambda b, i: (b, 0, 0, 0)),
            pl.BlockSpec((None, None, 8, D), lambda b, i: (b, 1, 0, 0))]


def _resident(a):
    return pl.BlockSpec(a.shape, lambda b, i: (0,) * a.ndim, pipeline_mode=pl.Buffered(1))


NSB = TM_CAT // TM


def _stream_tile(ctx_ref, lat_refs):
    first = jnp.where(pl.program_id(1) == 0, ctx_ref[...], lat_refs[0][...])
    return jnp.concatenate([first] + [r[...] for r in lat_refs[1:]], axis=0)


def _stream_specs():
    lat = lambda j: pl.BlockSpec((None, TM, D), lambda b, i: (b, jnp.maximum(NSB * i - 1 + j, 0), 0))
    return [pl.BlockSpec((None, TM, D), lambda b, i: (b, 0, 0))] + [lat(j) for j in range(NSB)]


def _even_in_kernel(*refs):
    c_ref, l_refs = refs[0], refs[1:1 + NSB]
    g_ref, modc_ref, modl_ref, wf_ref, wz_ref, wx_ref, wd_ref, f_ref, z_ref, x_ref, dt_ref = refs[1 + NSB:]
    x = _stream_tile(c_ref, l_refs)
    mod = _row_mods(modc_ref, modl_ref, TM_CAT, True)
    a = (_rms(x, g_ref[...]) * (1.0 + mod(1)) + mod(0)).astype(BF16)
    f_ref[...] = _dot(a, wf_ref[...])
    z_ref[...] = _dot(a, wz_ref[...]).astype(BF16)
    x_ref[...] = _dot(a, wx_ref[...]).astype(BF16)
    dt_ref[...] = _dot(a, wd_ref[...])


def _even_in(ctx, x, g, modtab, wf, wz, wx, wd):
    row = lambda n: pl.BlockSpec((None, TM_CAT, n), lambda b, i: (b, i, 0))
    full = lambda a: pl.BlockSpec(a.shape, lambda b, i: (0,) * a.ndim)
    return pl.pallas_call(
        _even_in_kernel,
        grid=(B, TT // TM_CAT),
        in_specs=_stream_specs() + [full(g)] + _mod_specs() + [
            _resident(wf), _resident(wz), _resident(wx), _resident(wd)],
        out_specs=[row(FW), row(SSD_INNER), row(SSD_XBC), row(DT_PAD)],
        out_shape=[jax.ShapeDtypeStruct((B, TT, n), dt) for n, dt in
                   ((FW, F32), (SSD_INNER, BF16), (SSD_XBC, BF16), (DT_PAD, F32))],
        compiler_params=_cp(("parallel", "parallel"), 58),
        name="even_in_proj",
    )(ctx, *([x] * NSB), g, modtab, modtab, wf, wz, wx, wd)


CONV_HALO = 16


def _conv5_kernel(x_ref, w_ref, b_ref, o_ref):
    w = w_ref[...]
    bias = b_ref[...]

    def body(c, carry):
        r0 = pl.multiple_of(c * TM, TM)
        center = x_ref[pl.ds(r0, TM), :].astype(F32)
        lo = pl.multiple_of(jnp.maximum(r0 - CONV_HALO, 0), CONV_HALO)
        hi = pl.multiple_of(jnp.minimum(r0 + TM, TT - CONV_HALO), CONV_HALO)
        left = jnp.where(c >= 2, x_ref[pl.ds(lo, CONV_HALO), :].astype(F32), 0.0)
        right = jnp.where(jnp.logical_and(c >= 1, c < NT_CAT - 1),
                          x_ref[pl.ds(hi, CONV_HALO), :].astype(F32), 0.0)
        ext = jnp.concatenate([left, center, right], axis=0)
        acc = bias + w[2:3, :] * center
        for k in (0, 1, 3, 4):
            off = CONV_HALO - SSD_K // 2 + k
            acc = acc + w[k:k + 1, :] * ext[off:off + TM, :]
        o_ref[pl.ds(r0, TM), :] = _silu(acc).astype(BF16)
        return carry

    lax.fori_loop(0, NT_CAT, body, 0)


def _conv5(xbc, w8, bias):
    return pl.pallas_call(
        _conv5_kernel,
        grid=(B, SSD_XBC // 128),
        in_specs=[
            pl.BlockSpec((None, TT, 128), lambda b, c: (b, 0, c)),
            pl.BlockSpec((8, 128), lambda b, c: (0, c)),
            pl.BlockSpec((1, 128), lambda b, c: (0, c)),
        ],
        out_specs=pl.BlockSpec((None, TT, 128), lambda b, c: (b, 0, c)),
        out_shape=jax.ShapeDtypeStruct((B, TT, SSD_XBC), BF16),
        compiler_params=_cp(("parallel", "parallel")),
        name="ssd_conv5",
    )(xbc, w8, bias)


def _ssd_bwd_chunk(i):
    return jnp.where(i < NCH_CTX, NCH_CTX - 1 - i, NCH_CAT + NCH_CTX - 1 - i)


SSD_NH = SSD_G * SSD_HPG


def _ssd_expand_table():
    lane = np.arange(DT_PAD)[:, None]
    col = np.arange(SSD_INNER)[None, :]
    tabs = []
    for d in range(2):
        e = (lane == d * SSD_NH + col // SSD_P).astype(np.float32)
        tabs.append(np.concatenate([e, e], axis=0))
    return jnp.asarray(np.stack(tabs), dtype=BF16)


def _hi_mid(x):
    hi = x.astype(BF16)
    mid = (x - hi.astype(F32)).astype(BF16)
    return jnp.concatenate([hi, mid], axis=1)


def _ssd_kernel(xf_ref, bf_ref, cf_ref, dtf_ref, xb_ref, bb_ref, cb_ref, dtb_ref, bias_ref, alog_ref,
                ep_ref, yf_ref, yb_ref, st_ref):
    @pl.when(pl.program_id(1) == 0)
    def _():
        st_ref[...] = jnp.zeros_like(st_ref)

    row = lax.broadcasted_iota(jnp.int32, (SSD_L, SSD_L), 0)
    col = lax.broadcasted_iota(jnp.int32, (SSD_L, SSD_L), 1)
    lane_lo = lax.broadcasted_iota(jnp.int32, (SSD_L, 128), 1) < SSD_P
    bias = bias_ref[...]
    a_all = -jnp.exp(alog_ref[...]) * math.log2(math.e)
    dirs = ((xf_ref, bf_ref, cf_ref, dtf_ref, yf_ref), (xb_ref, bb_ref, cb_ref, dtb_ref, yb_ref))

    for d, (xs_ref, bm_ref, cm_ref, dt_ref, y_ref) in enumerate(dirs):
        tri = (row >= col) if d == 0 else (row <= col)
        tri_b = jnp.where(tri, 1.0, 0.0).astype(BF16)
        raw = dt_ref[...] + bias
        dt_all = jnp.maximum(raw, 0.0) + jnp.log1p(jnp.exp(-jnp.abs(raw)))
        acs = _dot3_l(tri_b, dt_all * a_all)
        acs_t = acs.T
        dt_e = _dot(_hi_mid(dt_all), ep_ref[d])
        acs_e = _dot(_hi_mid(acs), ep_ref[d])
        last = SSD_L - 1 if d == 0 else 0
        tot_e = acs_e[last:last + 1, :]

        def head_mat(cb, h):
            c = d * SSD_NH + h
            seg = acs[:, c:c + 1] - acs_t[c:c + 1, :]
            dec = jnp.exp2(jnp.where(tri, seg, -jnp.inf))
            return (cb * dec).astype(BF16)

        for g in range(SSD_G):
            cm_b = cm_ref[:, g * SSD_N:(g + 1) * SSD_N]
            bm = bm_ref[:, g * SSD_N:(g + 1) * SSD_N]
            cb = _dot_nt(cm_b, bm)
            bm_t = bm.astype(F32).T.astype(BF16)
            glanes = slice(g * GW, (g + 1) * GW)
            acs_g = acs_e[:, glanes]
            tot_g = tot_e[:, glanes]
            xdt = xs_ref[:, glanes].astype(F32) * dt_e[:, glanes]
            st_old = st_ref[d, g]
            y_off = _dot(cm_b, st_old.astype(BF16)) * jnp.exp2(acs_g)
            st_ref[d, g] = st_old * jnp.exp2(tot_g) + _dot(bm_t, (xdt * jnp.exp2(tot_g - acs_g)).astype(BF16))
            for j in range(SSD_HPG // 2):
                h0 = g * SSD_HPG + 2 * j
                x_p = xdt[:, j * 128:(j + 1) * 128]
                x2 = jnp.concatenate([jnp.where(lane_lo, x_p, 0.0), jnp.where(lane_lo, 0.0, x_p)], axis=0)
                m2 = jnp.concatenate([head_mat(cb, h0), head_mat(cb, h0 + 1)], axis=1)
                y_diag = _dot(m2, x2.astype(BF16))
                lanes = slice(g * GW + j * 128, g * GW + (j + 1) * 128)
                y_ref[:, lanes] = (y_diag + y_off[:, j * 128:(j + 1) * 128]).astype(BF16)


def _ssd(xbc_act, dt, bias_all, alog_all):
    nbc = SSD_INNER // (SSD_G * SSD_N)
    ep = _ssd_expand_table()
    fwd = lambda blk: (lambda b, i: (b, i, blk))
    bwd = lambda blk: (lambda b, i: (b, _ssd_bwd_chunk(i), blk))
    specs = lambda ix: [pl.BlockSpec((None, SSD_L, SSD_INNER), ix(0)),
                        pl.BlockSpec((None, SSD_L, SSD_G * SSD_N), ix(nbc)),
                        pl.BlockSpec((None, SSD_L, SSD_G * SSD_N), ix(nbc + 1)),
                        pl.BlockSpec((None, SSD_L, DT_PAD), ix(0))]
    par = pl.BlockSpec((1, DT_PAD), lambda b, i: (0, 0))
    full = lambda a: pl.BlockSpec(a.shape, lambda b, i: (0,) * a.ndim)
    return pl.pallas_call(
        _ssd_kernel,
        grid=(B, NCH_CAT),
        in_specs=specs(fwd) + specs(bwd) + [par, par, full(ep)],
        out_specs=[pl.BlockSpec((None, SSD_L, SSD_INNER), fwd(0)),
                   pl.BlockSpec((None, SSD_L, SSD_INNER), bwd(0))],
        out_shape=[jax.ShapeDtypeStruct((B, TT, SSD_INNER), BF16)] * 2,
        scratch_shapes=[pltpu.VMEM((2, SSD_G, SSD_N, GW), F32)],
        compiler_params=_cp(("parallel", "arbitrary")),
        name="ssd_scan",
    )(xbc_act, xbc_act, xbc_act, dt, xbc_act, xbc_act, xbc_act, dt, bias_all, alog_all, ep)


def _dft_cs(n):
    k = np.arange(n, dtype=np.int64)
    ang = 2.0 * np.pi * ((k[:, None] * k[None, :]) % n).astype(np.float64) / n
    return np.cos(ang), np.sin(ang)


def _hi_lo(a):
    a32 = jnp.asarray(a, dtype=F32)
    hi = a32.astype(BF16)
    lo = (a32 - hi.astype(F32)).astype(BF16)
    return jnp.stack([hi, lo], axis=0)


def _fourier_tables():
    c1, s1 = _dft_cs(FT1)
    w1 = np.concatenate([c1, -s1], axis=0)
    r = np.arange(FT2)
    t2_of_r = F_OCT * (r % (FT2 // F_OCT)) + r // (FT2 // F_OCT)
    c2, s2 = _dft_cs(FT2)
    c2, s2 = c2[:, t2_of_r], s2[:, t2_of_r]
    w2 = np.block([[c2, s2], [-s2, c2]])
    cc, sc = _dft_cs(FGD)
    wc = np.concatenate([cc, sc], axis=0) / math.sqrt(T * FGD)
    ang = 2.0 * np.pi * t2_of_r.astype(np.float64)[:, None] / T
    tw = np.stack([np.cos(ang), np.sin(ang)], axis=0) * np.ones((1, 1, FGD))
    cx, sx = _dft_cs(TC)
    wx = np.concatenate([cx, sx], axis=0)
    wcx = np.concatenate([cc, -sc], axis=0) / math.sqrt(TC * FGD)
    return _hi_lo(w1), _hi_lo(w2), _hi_lo(wc), jnp.asarray(tw, dtype=F32), _hi_lo(wx), _hi_lo(wcx)


def _dot_hl(a_hl, x):
    xh = x.astype(BF16)
    xl = (x - xh.astype(F32)).astype(BF16)
    return _dot(a_hl[0], xh) + _dot(a_hl[0], xl) + _dot(a_hl[1], xh)


def _dot_lh(x, b_hl):
    xh = x.astype(BF16)
    xl = (x - xh.astype(F32)).astype(BF16)
    return _dot(xh, b_hl[0]) + _dot(xl, b_hl[0]) + _dot(xh, b_hl[1])


F_OCT = 8


def _fourier_kernel(x_ref, w1_ref, w2_ref, wc_ref, tw_ref, wx_ref, wcx_ref, o_ref, y1_ref):
    w1 = (w1_ref[0], w1_ref[1])
    w2 = (w2_ref[0], w2_ref[1])
    wc = (wc_ref[0], wc_ref[1])

    noct = FT2 // F_OCT
    for j in range(noct):
        cols = [x_ref[pl.ds(TC + F_OCT * j + q, FT1, stride=FT2), :] for q in range(F_OCT)]
        y = _dot_hl(w1, jnp.concatenate(cols, axis=1))
        for q in range(F_OCT):
            y1_ref[q, j * 2 * FT1:(j + 1) * 2 * FT1, :] = y[:, q * FGD:(q + 1) * FGD]

    def gather(kk):
        return jnp.concatenate([y1_ref[q, pl.ds(kk, noct, stride=2 * FT1), :] for q in range(F_OCT)], axis=0)

    cstep = tw_ref[0]
    sstep = tw_ref[1]

    def rotate(c, s):
        return c * cstep - s * sstep, s * cstep + c * sstep

    def twiddled(k1, c, s):
        yr = gather(k1)
        yi = gather(FT1 + k1)
        return yr * c + yi * s, yi * c - yr * s

    def body(i, carry):
        c0, s0 = carry
        c1, s1 = rotate(c0, s0)
        k1 = 2 * i
        re0, im0 = twiddled(k1, c0, s0)
        re1, im1 = twiddled(k1 + 1, c1, s1)
        y2 = jnp.concatenate([jnp.concatenate([re0, re1], axis=1),
                              jnp.concatenate([im0, im1], axis=1)], axis=0)
        y3 = _dot_hl(w2, y2)
        lhs = jnp.concatenate([jnp.concatenate([y3[:FT2, :FGD], y3[FT2:, :FGD]], axis=1),
                               jnp.concatenate([y3[:FT2, FGD:], y3[FT2:, FGD:]], axis=1)], axis=0)
        res = _dot_lh(lhs, wc)
        o_ref[pl.ds(TC + k1, FT2, stride=FT1), :] = res[:FT2, :]
        o_ref[pl.ds(TC + k1 + 1, FT2, stride=FT1), :] = res[FT2:, :]
        return rotate(c1, s1)

    init = (jnp.ones((FT2, FGD), F32), jnp.zeros((FT2, FGD), F32))
    lax.fori_loop(0, FT1 // 2, body, init, unroll=8)

    ab = _dot_hl((wx_ref[0], wx_ref[1]), x_ref[0:TC, :])
    lhs = jnp.concatenate([ab[:TC, :], ab[TC:, :]], axis=1)
    o_ref[0:TC, :] = _dot_lh(lhs, (wcx_ref[0], wcx_ref[1]))


def _fourier(f_cat):
    w1, w2, wc, tw, wx, wcx = _fourier_tables()
    full = lambda a: pl.BlockSpec(a.shape, lambda b, g: (0,) * a.ndim)
    blk = pl.BlockSpec((None, TT, FGD), lambda b, g: (b, 0, g))
    return pl.pallas_call(
        _fourier_kernel,
        grid=(B, FG),
        in_specs=[blk, full(w1), full(w2), full(wc), full(tw), full(wx), full(wcx)],
        out_specs=blk,
        out_shape=jax.ShapeDtypeStruct((B, TT, FW), F32),
        scratch_shapes=[pltpu.VMEM((F_OCT, FT2 // F_OCT * 2 * FT1, FGD), F32)],
        compiler_params=_cp(("parallel", "parallel")),
        name="fourier_mix",
    )(f_cat, w1, w2, wc, tw, wx, wcx)


def _even_out_kernel(*refs):
    c_ref, l_refs = refs[0], refs[1:1 + NSB]
    (fo_ref, yf_ref, yb_ref, xs_ref, z_ref, gn_ref, ds_ref, wf_ref, wy_ref, g_ref, modc_ref, modl_ref,
     o_ref) = refs[1 + NSB:]
    y = yf_ref[...].astype(F32) + yb_ref[...].astype(F32) + ds_ref[...] * xs_ref[...].astype(F32)
    gated = y * _silu(z_ref[...].astype(F32))
    gn = gn_ref[...]
    parts = []
    for g in range(SSD_G):
        parts.append(_rms(gated[:, g * GW:(g + 1) * GW], gn[:, g * GW:(g + 1) * GW]))
    yn = jnp.concatenate(parts, axis=1).astype(BF16)
    o = _dot(fo_ref[...].astype(BF16), wf_ref[...]) + _dot(yn, wy_ref[...])
    gate = _row_mods(modc_ref, modl_ref, TM_CAT, True)(2)
    o_ref[...] = _stream_tile(c_ref, l_refs) + gate * _rms(o, g_ref[...])


def _even_out(ctx, x, fo, yf, yb, xbc_act, z, gn, dsum, wf, wy, g, modtab):
    row = lambda n: pl.BlockSpec((None, TM_CAT, n), lambda b, i: (b, i, 0))
    full = lambda a: pl.BlockSpec(a.shape, lambda b, i: (0,) * a.ndim)
    return pl.pallas_call(
        _even_out_kernel,
        grid=(B, TT // TM_CAT),
        in_specs=_stream_specs() + [row(FW), row(SSD_INNER), row(SSD_INNER),
                                    row(SSD_INNER), row(SSD_INNER), full(gn), full(dsum),
                                    _resident(wf), _resident(wy), full(g)] + _mod_specs(),
        out_specs=row(D),
        out_shape=jax.ShapeDtypeStruct((B, TT, D), F32),
        compiler_params=_cp(("parallel", "parallel"), 58),
        name="even_out_proj",
    )(ctx, *([x] * NSB), fo, yf, yb, xbc_act, z, gn, dsum, wf, wy, g, modtab, modtab)


def _ffn_kernel(s_ref, g2_ref, g3_ref, modc_ref, modl_ref, wg_ref, wu_ref, wd_ref, o_ref, *, cat):
    x = s_ref[...]
    mod = _row_mods(modc_ref, modl_ref, x.shape[0], cat)
    a = (_rms(x, g2_ref[...]) * (1.0 + mod(4)) + mod(3)).astype(BF16)
    gt = _dot(a, wg_ref[...])
    up = _dot(a, wu_ref[...])
    hid = (_silu(gt) * up).astype(BF16)
    f = _dot(hid, wd_ref[...])
    o_ref[...] = x + mod(5) * _rms(f, g3_ref[...])


def _ffn(s, g2, g3, modtab, wg, wu, wd, cat):
    tm = TM_FFN_CAT if cat else TM_LAT
    rows = TT if cat else T
    row = pl.BlockSpec((None, tm, D), lambda b, i: (b, i, 0))
    full = lambda a: pl.BlockSpec(a.shape, lambda b, i: (0,) * a.ndim)
    mod_specs = _mod_specs()
    mixed = cat and tm != TC
    if cat and not mixed:
        mod_specs[1] = pl.BlockSpec((None, None, 8, D), lambda b, i: (b, jnp.minimum(i, 1), 0, 0))
    return pl.pallas_call(
        functools.partial(_ffn_kernel, cat=mixed),
        grid=(B, rows // tm),
        in_specs=[row, full(g2), full(g3)] + mod_specs + [_resident(wg), _resident(wu), _resident(wd)],
        out_specs=row,
        out_shape=jax.ShapeDtypeStruct((B, rows, D), F32),
        compiler_params=_cp(("parallel", "parallel"), 60),
        name="ffn_cat" if cat else "ffn_lat",
    )(s, g2, g3, modtab, modtab, wg, wu, wd)


def _rope_tables():
    rows = T // GRID_W
    nf = QK // 4
    inv_freq = ROPE_BASE ** (-jnp.arange(nf, dtype=F32) * 2.0 / (QK // 2))
    lane = np.arange(128)
    axis = (lane % QK) // (QK // 2)
    half = (lane % (QK // 2)) // nf
    freq = lane % nf
    ang_r = (jnp.arange(rows, dtype=F32)[:, None] * inv_freq)[:, freq]
    ang_c = (jnp.arange(GRID_W, dtype=F32)[:, None] * inv_freq)[:, freq]
    by_row = jnp.asarray(axis == 0)[None, None, :]

    def per_token(fn):
        tab = jnp.where(by_row, fn(ang_r)[:, None, :], fn(ang_c)[None, :, :])
        return tab.reshape(T, 128)

    cos_l = per_token(jnp.cos)
    sin_l = per_token(jnp.sin)
    sa = jnp.where(jnp.asarray(half == 0)[None, :], -sin_l, 0.0)
    sb = jnp.where(jnp.asarray(half == 1)[None, :], sin_l, 0.0)
    ctx1 = jnp.ones((TC, 128), F32)
    ctx0 = jnp.zeros((TC, 128), F32)
    return (jnp.concatenate([ctx1, cos_l], axis=0), jnp.concatenate([ctx0, sa], axis=0),
            jnp.concatenate([ctx0, sb], axis=0))


def _odd_in_kernel(s_ref, g_ref, modc_ref, modl_ref, wq_ref, wk_ref, wv_ref, wu_ref, cos_ref, sa_ref, sb_ref,
                   q_ref, k_ref, v_ref, u_ref):
    x = s_ref[...]
    mod = _row_mods(modc_ref, modl_ref, x.shape[0], True)
    a = (_rms(x, g_ref[...]) * (1.0 + mod(1)) + mod(0)).astype(BF16)
    cos = cos_ref[...]
    sa = sa_ref[...]
    sb = sb_ref[...]

    def rope(x, scale):
        outs = []
        for h in range(NH):
            xh = x[:, h * 128:(h + 1) * 128]
            r = xh * cos + pltpu.roll(xh, 112, 1) * sa + pltpu.roll(xh, 16, 1) * sb
            outs.append(r * scale if scale != 1.0 else r)
        return jnp.concatenate(outs, axis=1)

    q_ref[...] = rope(_dot(a, wq_ref[...]), ATTN_SCALE * math.log2(math.e)).astype(BF16)
    k_ref[...] = rope(_dot(a, wk_ref[...]), 1.0).astype(BF16)
    v_ref[...] = _dot(a, wv_ref[...]).astype(BF16)
    u_ref[...] = _dot(a, wu_ref[...]).astype(BF16)


def _odd_in(s, g, modtab, wq, wk, wv, wu, cos, sa, sb):
    row = lambda n: pl.BlockSpec((None, TM_CAT, n), lambda b, i: (b, i, 0))
    full = lambda a: pl.BlockSpec(a.shape, lambda b, i: (0,) * a.ndim)
    tab = pl.BlockSpec((TM_CAT, 128), lambda b, i: (i, 0))
    return pl.pallas_call(
        _odd_in_kernel,
        grid=(B, TT // TM_CAT),
        in_specs=[row(D), full(g)] + _mod_specs() + [
                  _resident(wq), _resident(wk), _resident(wv), _resident(wu), tab, tab, tab],
        out_specs=[row(D), row(D), row(D), row(D)],
        out_shape=[jax.ShapeDtypeStruct((B, TT, D), BF16)] * 4,
        compiler_params=_cp(("parallel", "parallel"), 56),
        name="odd_in_proj",
    )(s, g, modtab, modtab, wq, wk, wv, wu, cos, sa, sb)


def _attn_kernel(*refs, lambda_init):
    nqb = TQ // TM
    q_refs = refs[:3 * nqb]
    k_ref, v_ref, lam_ref, sg_ref, o_ref, q2_ref, s_ref, mp_ref, mb_ref, ls_ref, acc_ref = refs[3 * nqb:]
    i = pl.program_id(2)
    lane_lo = lax.broadcasted_iota(jnp.int32, (TM, 128), 1) < QK

    for slot in range(3):
        for j in range(nqb):
            q = q_refs[slot * nqb + j][...]
            zero = jnp.zeros_like(q)
            q2_ref[slot, j * TM:(j + 1) * TM, :] = jnp.where(lane_lo, q, zero)
            q2_ref[slot, TQ + j * TM:TQ + (j + 1) * TM, :] = jnp.where(lane_lo, zero, q)

    def produce(c, qslot):
        r0 = c * KT if isinstance(c, int) else pl.multiple_of(c * KT, KT)
        s = _dot_nt(q2_ref[qslot], k_ref[pl.ds(r0, KT), :])
        s_ref[c] = s
        mp_ref[...] = jnp.maximum(mp_ref[...], jnp.maximum(s[:, 0:128], s[:, 128:256]))

    def finish_max():
        m = jnp.max(mp_ref[...], axis=-1, keepdims=True)
        mb_ref[...] = jnp.broadcast_to(m, mb_ref.shape)
        mp_ref[...] = jnp.full(mp_ref.shape, -jnp.inf, F32)

    def consume(c):
        r0 = c * KT if isinstance(c, int) else pl.multiple_of(c * KT, KT)
        s = s_ref[c]
        mb = mb_ref[...]
        p0 = jnp.exp2(s[:, 0:128] - mb)
        p1 = jnp.exp2(s[:, 128:256] - mb)
        ls_ref[...] += p0 + p1
        p = jnp.concatenate([p0, p1], axis=1).astype(BF16)
        acc_ref[...] += _dot(p, v_ref[pl.ds(r0, KT), :])

    lf = lam_ref[...]
    lam = (jnp.exp(jnp.sum(lf[0:1, :] * lf[1:2, :], axis=-1, keepdims=True))
           - jnp.exp(jnp.sum(lf[2:3, :] * lf[3:4, :], axis=-1, keepdims=True)) + lambda_init)

    def phase(next_q, out_rows):
        ls_ref[...] = jnp.zeros(ls_ref.shape, F32)
        acc_ref[...] = jnp.zeros(acc_ref.shape, F32)

        def body(c, carry):
            consume(c)
            produce(c, next_q)
            return carry

        lax.fori_loop(0, NKT, body, 0, unroll=ATT_UNROLL)
        finish_max()
        o = acc_ref[...] / jnp.sum(ls_ref[...], axis=-1, keepdims=True)
        od = o[0:TQ, :] - lam * o[TQ:2 * TQ, :]
        o_ref[out_rows, :] = (_rms(od, sg_ref[...]) * (1.0 - lambda_init)).astype(BF16)

    @pl.when(i == 0)
    def _():
        mp_ref[...] = jnp.full(mp_ref.shape, -jnp.inf, F32)

        def body(c, carry):
            produce(c, 0)
            return carry

        lax.fori_loop(0, NKT, body, 0, unroll=ATT_UNROLL)
        finish_max()

    phase(1, slice(0, TQ))
    phase(2, slice(TQ, 2 * TQ))


def _attention(q, k, v, lam, subln_g, lambda_init):
    nqb = TQ // TM
    nsteps = T // (2 * TQ)
    qspec = lambda j: pl.BlockSpec((None, TM, 128),
                                   lambda b, h, i: (b, jnp.minimum(2 * nqb * i + 1 + j, NT_LAT), h))
    return pl.pallas_call(
        functools.partial(_attn_kernel, lambda_init=lambda_init),
        grid=(B, NH, nsteps),
        in_specs=[qspec(j) for j in range(3 * nqb)] + [
                  pl.BlockSpec((None, TT, 128), lambda b, h, i: (b, 0, h)),
                  pl.BlockSpec((None, TT, 128), lambda b, h, i: (b, 0, h)),
                  pl.BlockSpec((4, QK), lambda b, h, i: (0, 0)),
                  pl.BlockSpec((1, DV), lambda b, h, i: (0, 0))],
        out_specs=pl.BlockSpec((None, 2 * TQ, 128), lambda b, h, i: (b, i, h)),
        out_shape=jax.ShapeDtypeStruct((B, T, NH * DV), BF16),
        scratch_shapes=[pltpu.VMEM((3, 2 * TQ, 128), BF16),
                        pltpu.VMEM((NKT, 2 * TQ, KT), F32),
                        pltpu.VMEM((2 * TQ, 128), F32),
                        pltpu.VMEM((2 * TQ, 128), F32),
                        pltpu.VMEM((2 * TQ, 128), F32),
                        pltpu.VMEM((2 * TQ, DV), F32)],
        compiler_params=_cp(("parallel", "parallel", "arbitrary"), 58),
        name="diff_attention",
    )(*([q] * (3 * nqb)), k, v, lam, subln_g)


CONF_HALO = 16


def _conf_conv_kernel(a_ref, gt_ref, w_ref, b_ref, o_ref, vs_ref, sh_ref):
    def fill(c, carry):
        r0 = pl.multiple_of(c * TM, TM)
        v = a_ref[pl.ds(r0, TM), :].astype(F32) * _sigmoid(gt_ref[pl.ds(r0, TM), :].astype(F32))
        vs_ref[pl.ds(r0, TM), :] = jnp.where(c >= 1, v, 0.0)
        return carry

    lax.fori_loop(0, NT_CAT, fill, 0)
    vs_ref[TT:TT + CONF_HALO, :] = jnp.zeros((CONF_HALO, 128), F32)
    w = w_ref[...]
    bias = b_ref[...]

    def body(c, carry):
        start = pl.multiple_of(c * TM + (TC - CONF_HALO), 8)
        win = vs_ref[pl.ds(start, TM + 2 * CONF_HALO), :]
        acc = jnp.broadcast_to(bias, (TM, 128))
        for s in range(8):
            sh_ref[...] = win[s:s + TM + CONF_HALO + 8, :]
            for k in range(CONF_K):
                off = CONF_HALO - CONF_K // 2 + k
                if off % 8 == s:
                    acc = acc + w[k:k + 1, :] * sh_ref[off - s:off - s + TM, :]
        o_ref[pl.ds(pl.multiple_of(c * TM, TM), TM), :] = acc
        return carry

    lax.fori_loop(0, NT_LAT, body, 0)


def _conf_conv(u, w32, bias):
    nct = CONF_C // 128
    return pl.pallas_call(
        _conf_conv_kernel,
        grid=(B, nct),
        in_specs=[pl.BlockSpec((None, TT, 128), lambda b, c: (b, 0, c)),
                  pl.BlockSpec((None, TT, 128), lambda b, c: (b, 0, nct + c)),
                  pl.BlockSpec((32, 128), lambda b, c: (0, c)),
                  pl.BlockSpec((1, 128), lambda b, c: (0, c))],
        out_specs=pl.BlockSpec((None, T, 128), lambda b, c: (b, 0, c)),
        out_shape=jax.ShapeDtypeStruct((B, T, CONF_C), F32),
        scratch_shapes=[pltpu.VMEM((TT + CONF_HALO, 128), F32),
                        pltpu.VMEM((TM + CONF_HALO + 8, 128), F32)],
        compiler_params=_cp(("parallel", "parallel")),
        name="conformer_conv31",
    )(u, u, w32, bias)


def _odd_out_kernel(s0_ref, s1_ref, oa_ref, cv_ref, cg_ref, cb_ref, wa_ref, wc_ref, g_ref, mod_ref, o_ref):
    v = cv_ref[...]
    mu = jnp.mean(v, axis=-1, keepdims=True)
    vc = v - mu
    var = jnp.mean(vc * vc, axis=-1, keepdims=True)
    cv = _silu(vc * lax.rsqrt(var + EPS) * cg_ref[...] + cb_ref[...])
    o = _dot(oa_ref[...], wa_ref[...]) + _dot(cv.astype(BF16), wc_ref[...])
    resid = jnp.concatenate([s0_ref[...], s1_ref[...]], axis=0)
    o_ref[...] = resid + mod_ref[2:3, :] * _rms(o, g_ref[...])


def _odd_out(s_cat, oa, cvpre, cg, cb, wa, wc, g, modtab):
    full = lambda a: pl.BlockSpec(a.shape, lambda b, i: (0,) * a.ndim)
    row = lambda n: pl.BlockSpec((None, TM_LAT, n), lambda b, i: (b, i, 0))
    assert TM_LAT == 2 * TM
    return pl.pallas_call(
        _odd_out_kernel,
        grid=(B, T // TM_LAT),
        in_specs=[pl.BlockSpec((None, TM, D), lambda b, i: (b, 2 * i + 1, 0)),
                  pl.BlockSpec((None, TM, D), lambda b, i: (b, 2 * i + 2, 0)),
                  row(NH * DV), row(CONF_C),
                  full(cg), full(cb), full(wa), full(wc), full(g),
                  pl.BlockSpec((None, None, 8, D), _lat_mod_idx)],
        out_specs=row(D),
        out_shape=jax.ShapeDtypeStruct((B, T, D), F32),
        compiler_params=_cp(("parallel", "parallel")),
        name="odd_out_proj",
    )(s_cat, s_cat, oa, cvpre, cg, cb, wa, wc, g, modtab)


def _pad_rows(a, n):
    return jnp.pad(a, ((0, n - a.shape[0]), (0, 0)))


def _pad_lanes(a, n):
    return jnp.pad(a, ((0, 0), (0, n - a.shape[1])))


def kernel(x, c, ctx, c_ctx, mod_w, mod_b, norm_g, ffn_w_gate, ffn_w_up, ffn_w_down, ev_w_in, ev_conv_w, ev_conv_b, ev_dt_bias, ev_a_log, ev_d_skip, ev_gnorm_g, ev_w_out, od_w_in, od_lambda, od_subln_g, od_conv_w, od_conv_b, od_cnorm_g, od_cnorm_b, od_w_out):
    bf = lambda a: a.astype(BF16)

    cvec = _pad_rows(jnp.concatenate([c, c_ctx[None, :]], axis=0), 8)
    mv = _mod_vectors(cvec, mod_w, mod_b).reshape(DEPTH, 8, N_MOD, D)
    modtabs = []
    for l in range(DEPTH):
        lat = mv[l, :B]
        cx = jnp.broadcast_to(mv[l, B][None], (B, N_MOD, D))
        tab = jnp.stack([cx, lat], axis=1)
        modtabs.append(jnp.pad(tab, ((0, 0), (0, 0), (0, 8 - N_MOD), (0, 0))))

    g = norm_g[0]
    w_in = ev_w_in[0]
    o0, o1, o2 = FW, FW + SSD_INNER, FW + SSD_INNER + SSD_XBC
    f, z, xbc, dt = _even_in(ctx, x, g[0:1], modtabs[0], bf(w_in[:, :o0]), bf(w_in[:, o0:o1]),
                             bf(w_in[:, o1:o2]), bf(_pad_lanes(w_in[:, o2:], DT_PAD)))
    xbc_act = _conv5(xbc, _pad_rows(ev_conv_w[0], 8), ev_conv_b[0][None, :])
    bias_all = _pad_lanes(ev_dt_bias[0].reshape(1, -1), DT_PAD)
    alog_all = _pad_lanes(ev_a_log[0].reshape(1, -1), DT_PAD)
    yf, yb = _ssd(xbc_act, dt, bias_all, alog_all)
    fo = _fourier(f)
    dsum = jnp.repeat(ev_d_skip[0, 0] + ev_d_skip[0, 1], SSD_P)[None, :]
    w_out = ev_w_out[0]
    s = _even_out(ctx, x, fo, yf, yb, xbc_act, z, ev_gnorm_g[0][None, :], dsum, bf(w_out[:FW]), bf(w_out[FW:]),
                  g[1:2], modtabs[0])
    s = _ffn(s, g[2:3], g[3:4], modtabs[0], bf(ffn_w_gate[0]), bf(ffn_w_up[0]), bf(ffn_w_down[0]), True)

    g = norm_g[1]
    lambda_init = 0.8 - 0.6 * math.exp(-0.3 * 1)
    w_in = od_w_in[0]
    cos, sa, sb = _rope_tables()
    q, k, v, u = _odd_in(s, g[0:1], modtabs[1], bf(w_in[:, :D]), bf(w_in[:, D:2 * D]),
                         bf(w_in[:, 2 * D:3 * D]), bf(w_in[:, 3 * D:]), cos, sa, sb)
    oa = _attention(q, k, v, od_lambda[0], od_subln_g[0][None, :], lambda_init)
    cvpre = _conf_conv(u, _pad_rows(od_conv_w[0], 32), od_conv_b[0][None, :])
    w_out = od_w_out[0]
    h = _odd_out(s, oa, cvpre, od_cnorm_g[0][None, :], od_cnorm_b[0][None, :], bf(w_out[:NH * DV]),
                 bf(w_out[NH * DV:]), g[1:2], modtabs[1])
    return _ffn(h, g[2:3], g[3:4], modtabs[1], bf(ffn_w_gate[1]), bf(ffn_w_up[1]), bf(ffn_w_down[1]), False)
```

```python
import functools
import math

import numpy as np
import jax
import jax.numpy as jnp
from jax import lax
from jax.experimental import pallas as pl
from jax.experimental.pallas import tpu as pltpu

F32 = jnp.float32
BF16 = jnp.bfloat16
HIGHEST = lax.Precision.HIGHEST

D = 1024
B = 2
T = 8192
TC = 256
TT = TC + T
DEPTH = 2
N_MOD = 6
HID = 2816
EPS = 1e-6
GRID_W = 64

TM = 256
NT_CAT = TT // TM
NT_LAT = T // TM

FW = 512
FG = 4
FGD = 128
SSD_INNER = 1536
SSD_G = 4
SSD_HPG = 6
SSD_P = 64
SSD_N = 128
SSD_L = 128
SSD_XBC = 2560
SSD_K = 5
NCH_CAT = TT // SSD_L
NCH_CTX = TC // SSD_L
GW = SSD_HPG * SSD_P
DT_PAD = 128

NH = 8
QK = 64
DV = 128
ATTN_SCALE = QK ** -0.5
CONF_C = 512
CONF_K = 31
ROPE_BASE = 10000.0
TQ = 512
KT = 256
NKT = TT // KT
ATT_UNROLL = 11

FT1 = 64
FT2 = 128


def _cp(sem, vmem_mb=48):
    return pltpu.CompilerParams(dimension_semantics=sem, vmem_limit_bytes=vmem_mb * 1024 * 1024)


def _sigmoid(x):
    return 1.0 / (1.0 + jnp.exp(-x))


def _silu(x):
    return x * _sigmoid(x)


def _rms(x, g):
    ms = jnp.mean(x * x, axis=-1, keepdims=True)
    return x * lax.rsqrt(ms + EPS) * g


def _dot(a, b):
    return jnp.dot(a, b, preferred_element_type=F32)


def _dot_nt(a, b):
    return lax.dot_general(a, b, (((1,), (1,)), ((), ())), preferred_element_type=F32)


def _split3(x):
    hi = x.astype(BF16)
    r1 = x - hi.astype(F32)
    mid = r1.astype(BF16)
    lo = (r1 - mid.astype(F32)).astype(BF16)
    return hi, mid, lo


def _dot3_l(m01, x):
    hi, mid, lo = _split3(x)
    return _dot(m01, hi) + _dot(m01, mid) + _dot(m01, lo)


def _dot3_r(x, m01):
    hi, mid, lo = _split3(x)
    return _dot(hi, m01) + _dot(mid, m01) + _dot(lo, m01)


def _mod_kernel(c_ref, w_ref, b_ref, o_ref):
    a = _silu(c_ref[...])
    o_ref[...] = jnp.dot(a, w_ref[...], precision=HIGHEST, preferred_element_type=F32) + b_ref[...]


def _mod_vectors(cvec, mod_w, mod_b):
    tn = 1536
    return pl.pallas_call(
        _mod_kernel,
        grid=(DEPTH, N_MOD * D // tn),
        in_specs=[
            pl.BlockSpec((8, D), lambda l, j: (0, 0)),
            pl.BlockSpec((None, D, tn), lambda l, j: (l, 0, j)),
            pl.BlockSpec((None, 1, tn), lambda l, j: (l, 0, j)),
        ],
        out_specs=pl.BlockSpec((None, 8, tn), lambda l, j: (l, 0, j)),
        out_shape=jax.ShapeDtypeStruct((DEPTH, 8, N_MOD * D), F32),
        compiler_params=_cp(("parallel", "parallel")),
        name="mod_vectors",
    )(cvec, mod_w, mod_b.reshape(DEPTH, 1, N_MOD * D))


def _lat_mod_idx(b, i):
    return (b, 1, 0, 0)


TM_CAT = 768
TM_LAT = 512
TM_FFN_CAT = 256


def _row_mods(modc_ref, modl_ref, tm, cat):
    if not cat:
        return lambda r: modl_ref[r:r + 1, :]
    rows = lax.broadcasted_iota(jnp.int32, (tm, 1), 0) + pl.program_id(1) * tm
    is_ctx = rows < TC
    return lambda r: jnp.where(is_ctx, modc_ref[r:r + 1, :], modl_ref[r:r + 1, :])


def _mod_specs():
    return [pl.BlockSpec((None, None, 8, D), lambda b, i: (b, 0, 0, 0)),
            pl.BlockSpec((None, None, 8, D), lambda b, i: (b, 1, 0, 0))]


def _resident(a):
    return pl.BlockSpec(a.shape, lambda b, i: (0,) * a.ndim, pipeline_mode=pl.Buffered(1))


NSB = TM_CAT // TM


def _stream_tile(ctx_ref, lat_refs):
    first = jnp.where(pl.program_id(1) == 0, ctx_ref[...], lat_refs[0][...])
    return jnp.concatenate([first] + [r[...] for r in lat_refs[1:]], axis=0)


def _stream_specs():
    lat = lambda j: pl.BlockSpec((None, TM, D), lambda b, i: (b, jnp.maximum(NSB * i - 1 + j, 0), 0))
    return [pl.BlockSpec((None, TM, D), lambda b, i: (b, 0, 0))] + [lat(j) for j in range(NSB)]


def _even_in_kernel(*refs):
    c_ref, l_refs = refs[0], refs[1:1 + NSB]
    g_ref, modc_ref, modl_ref, wf_ref, wz_ref, wx_ref, wd_ref, f_ref, z_ref, x_ref, dt_ref = refs[1 + NSB:]
    x = _stream_tile(c_ref, l_refs)
    mod = _row_mods(modc_ref, modl_ref, TM_CAT, True)
    a = (_rms(x, g_ref[...]) * (1.0 + mod(1)) + mod(0)).astype(BF16)
    f_ref[...] = _dot(a, wf_ref[...])
    z_ref[...] = _dot(a, wz_ref[...]).astype(BF16)
    x_ref[...] = _dot(a, wx_ref[...]).astype(BF16)
    dt_ref[...] = _dot(a, wd_ref[...])


def _even_in(ctx, x, g, modtab, wf, wz, wx, wd):
    row = lambda n: pl.BlockSpec((None, TM_CAT, n), lambda b, i: (b, i, 0))
    full = lambda a: pl.BlockSpec(a.shape, lambda b, i: (0,) * a.ndim)
    return pl.pallas_call(
        _even_in_kernel,
        grid=(B, TT // TM_CAT),
        in_specs=_stream_specs() + [full(g)] + _mod_specs() + [
            _resident(wf), _resident(wz), _resident(wx), _resident(wd)],
        out_specs=[row(FW), row(SSD_INNER), row(SSD_XBC), row(DT_PAD)],
        out_shape=[jax.ShapeDtypeStruct((B, TT, n), dt) for n, dt in
                   ((FW, F32), (SSD_INNER, BF16), (SSD_XBC, BF16), (DT_PAD, F32))],
        compiler_params=_cp(("parallel", "parallel"), 58),
        name="even_in_proj",
    )(ctx, *([x] * NSB), g, modtab, modtab, wf, wz, wx, wd)


CONV_HALO = 16


def _conv5_kernel(x_ref, w_ref, b_ref, o_ref):
    w = w_ref[...]
    bias = b_ref[...]

    def body(c, carry):
        r0 = pl.multiple_of(c * TM, TM)
        center = x_ref[pl.ds(r0, TM), :].astype(F32)
        lo = pl.multiple_of(jnp.maximum(r0 - CONV_HALO, 0), CONV_HALO)
        hi = pl.multiple_of(jnp.minimum(r0 + TM, TT - CONV_HALO), CONV_HALO)
        left = jnp.where(c >= 2, x_ref[pl.ds(lo, CONV_HALO), :].astype(F32), 0.0)
        right = jnp.where(jnp.logical_and(c >= 1, c < NT_CAT - 1),
                          x_ref[pl.ds(hi, CONV_HALO), :].astype(F32), 0.0)
        ext = jnp.concatenate([left, center, right], axis=0)
        acc = bias + w[2:3, :] * center
        for k in (0, 1, 3, 4):
            off = CONV_HALO - SSD_K // 2 + k
            acc = acc + w[k:k + 1, :] * ext[off:off + TM, :]
        o_ref[pl.ds(r0, TM), :] = _silu(acc).astype(BF16)
        return carry

    lax.fori_loop(0, NT_CAT, body, 0)


def _conv5(xbc, w8, bias):
    return pl.pallas_call(
        _conv5_kernel,
        grid=(B, SSD_XBC // 128),
        in_specs=[
            pl.BlockSpec((None, TT, 128), lambda b, c: (b, 0, c)),
            pl.BlockSpec((8, 128), lambda b, c: (0, c)),
            pl.BlockSpec((1, 128), lambda b, c: (0, c)),
        ],
        out_specs=pl.BlockSpec((None, TT, 128), lambda b, c: (b, 0, c)),
        out_shape=jax.ShapeDtypeStruct((B, TT, SSD_XBC), BF16),
        compiler_params=_cp(("parallel", "parallel")),
        name="ssd_conv5",
    )(xbc, w8, bias)


def _ssd_bwd_chunk(i):
    return jnp.where(i < NCH_CTX, NCH_CTX - 1 - i, NCH_CAT + NCH_CTX - 1 - i)


SSD_NH = SSD_G * SSD_HPG


def _ssd_expand_table():
    lane = np.arange(DT_PAD)[:, None]
    col = np.arange(SSD_INNER)[None, :]
    tabs = []
    for d in range(2):
        e = (lane == d * SSD_NH + col // SSD_P).astype(np.float32)
        tabs.append(np.concatenate([e, e], axis=0))
    return jnp.asarray(np.stack(tabs), dtype=BF16)


def _hi_mid(x):
    hi = x.astype(BF16)
    mid = (x - hi.astype(F32)).astype(BF16)
    return jnp.concatenate([hi, mid], axis=1)


def _ssd_kernel(xf_ref, bf_ref, cf_ref, dtf_ref, xb_ref, bb_ref, cb_ref, dtb_ref, bias_ref, alog_ref,
                ep_ref, yf_ref, yb_ref, st_ref):
    @pl.when(pl.program_id(1) == 0)
    def _():
        st_ref[...] = jnp.zeros_like(st_ref)

    row = lax.broadcasted_iota(jnp.int32, (SSD_L, SSD_L), 0)
    col = lax.broadcasted_iota(jnp.int32, (SSD_L, SSD_L), 1)
    lane_lo = lax.broadcasted_iota(jnp.int32, (SSD_L, 128), 1) < SSD_P
    bias = bias_ref[...]
    a_all = -jnp.exp(alog_ref[...]) * math.log2(math.e)
    dirs = ((xf_ref, bf_ref, cf_ref, dtf_ref, yf_ref), (xb_ref, bb_ref, cb_ref, dtb_ref, yb_ref))

    for d, (xs_ref, bm_ref, cm_ref, dt_ref, y_ref) in enumerate(dirs):
        tri = (row >= col) if d == 0 else (row <= col)
        tri_b = jnp.where(tri, 1.0, 0.0).astype(BF16)
        raw = dt_ref[...] + bias
        dt_all = jnp.maximum(raw, 0.0) + jnp.log1p(jnp.exp(-jnp.abs(raw)))
        acs = _dot3_l(tri_b, dt_all * a_all)
        acs_t = acs.T
        dt_e = _dot(_hi_mid(dt_all), ep_ref[d])
        acs_e = _dot(_hi_mid(acs), ep_ref[d])
        last = SSD_L - 1 if d == 0 else 0
        tot_e = acs_e[last:last + 1, :]

        def head_mat(cb, h):
            c = d * SSD_NH + h
            seg = acs[:, c:c + 1] - acs_t[c:c + 1, :]
            dec = jnp.exp2(jnp.where(tri, seg, -jnp.inf))
            return (cb * dec).astype(BF16)

        for g in range(SSD_G):
            cm_b = cm_ref[:, g * SSD_N:(g + 1) * SSD_N]
            bm = bm_ref[:, g * SSD_N:(g + 1) * SSD_N]
            cb = _dot_nt(cm_b, bm)
            bm_t = bm.astype(F32).T.astype(BF16)
            glanes = slice(g * GW, (g + 1) * GW)
            acs_g = acs_e[:, glanes]
            tot_g = tot_e[:, glanes]
            xdt = xs_ref[:, glanes].astype(F32) * dt_e[:, glanes]
            st_old = st_ref[d, g]
            y_off = _dot(cm_b, st_old.astype(BF16)) * jnp.exp2(acs_g)
            st_ref[d, g] = st_old * jnp.exp2(tot_g) + _dot(bm_t, (xdt * jnp.exp2(tot_g - acs_g)).astype(BF16))
            for j in range(SSD_HPG // 2):
                h0 = g * SSD_HPG + 2 * j
                x_p = xdt[:, j * 128:(j + 1) * 128]
                x2 = jnp.concatenate([jnp.where(lane_lo, x_p, 0.0), jnp.where(lane_lo, 0.0, x_p)], axis=0)
                m2 = jnp.concatenate([head_mat(cb, h0), head_mat(cb, h0 + 1)], axis=1)
                y_diag = _dot(m2, x2.astype(BF16))
                lanes = slice(g * GW + j * 128, g * GW + (j + 1) * 128)
                y_ref[:, lanes] = (y_diag + y_off[:, j * 128:(j + 1) * 128]).astype(BF16)


def _ssd(xbc_act, dt, bias_all, alog_all):
    nbc = SSD_INNER // (SSD_G * SSD_N)
    ep = _ssd_expand_table()
    fwd = lambda blk: (lambda b, i: (b, i, blk))
    bwd = lambda blk: (lambda b, i: (b, _ssd_bwd_chunk(i), blk))
    specs = lambda ix: [pl.BlockSpec((None, SSD_L, SSD_INNER), ix(0)),
                        pl.BlockSpec((None, SSD_L, SSD_G * SSD_N), ix(nbc)),
                        pl.BlockSpec((None, SSD_L, SSD_G * SSD_N), ix(nbc + 1)),
                        pl.BlockSpec((None, SSD_L, DT_PAD), ix(0))]
    par = pl.BlockSpec((1, DT_PAD), lambda b, i: (0, 0))
    full = lambda a: pl.BlockSpec(a.shape, lambda b, i: (0,) * a.ndim)
    return pl.pallas_call(
        _ssd_kernel,
        grid=(B, NCH_CAT),
        in_specs=specs(fwd) + specs(bwd) + [par, par, full(ep)],
        out_specs=[pl.BlockSpec((None, SSD_L, SSD_INNER), fwd(0)),
                   pl.BlockSpec((None, SSD_L, SSD_INNER), bwd(0))],
        out_shape=[jax.ShapeDtypeStruct((B, TT, SSD_INNER), BF16)] * 2,
        scratch_shapes=[pltpu.VMEM((2, SSD_G, SSD_N, GW), F32)],
        compiler_params=_cp(("parallel", "arbitrary")),
        name="ssd_scan",
    )(xbc_act, xbc_act, xbc_act, dt, xbc_act, xbc_act, xbc_act, dt, bias_all, alog_all, ep)


def _dft_cs(n):
    k = np.arange(n, dtype=np.int64)
    ang = 2.0 * np.pi * ((k[:, None] * k[None, :]) % n).astype(np.float64) / n
    return np.cos(ang), np.sin(ang)


def _hi_lo(a):
    a32 = jnp.asarray(a, dtype=F32)
    hi = a32.astype(BF16)
    lo = (a32 - hi.astype(F32)).astype(BF16)
    return jnp.stack([hi, lo], axis=0)


def _fourier_tables():
    c1, s1 = _dft_cs(FT1)
    w1 = np.concatenate([c1, -s1], axis=0)
    r = np.arange(FT2)
    t2_of_r = F_OCT * (r % (FT2 // F_OCT)) + r // (FT2 // F_OCT)
    c2, s2 = _dft_cs(FT2)
    c2, s2 = c2[:, t2_of_r], s2[:, t2_of_r]
    w2 = np.block([[c2, s2], [-s2, c2]])
    cc, sc = _dft_cs(FGD)
    wc = np.concatenate([cc, sc], axis=0) / math.sqrt(T * FGD)
    ang = 2.0 * np.pi * t2_of_r.astype(np.float64)[:, None] / T
    tw = np.stack([np.cos(ang), np.sin(ang)], axis=0) * np.ones((1, 1, FGD))
    cx, sx = _dft_cs(TC)
    wx = np.concatenate([cx, sx], axis=0)
    wcx = np.concatenate([cc, -sc], axis=0) / math.sqrt(TC * FGD)
    return _hi_lo(w1), _hi_lo(w2), _hi_lo(wc), jnp.asarray(tw, dtype=F32), _hi_lo(wx), _hi_lo(wcx)


def _dot_hl(a_hl, x):
    xh = x.astype(BF16)
    xl = (x - xh.astype(F32)).astype(BF16)
    return _dot(a_hl[0], xh) + _dot(a_hl[0], xl) + _dot(a_hl[1], xh)


def _dot_lh(x, b_hl):
    xh = x.astype(BF16)
    xl = (x - xh.astype(F32)).astype(BF16)
    return _dot(xh, b_hl[0]) + _dot(xl, b_hl[0]) + _dot(xh, b_hl[1])


F_OCT = 8


def _fourier_kernel(x_ref, w1_ref, w2_ref, wc_ref, tw_ref, wx_ref, wcx_ref, o_ref, y1_ref):
    w1 = (w1_ref[0], w1_ref[1])
    w2 = (w2_ref[0], w2_ref[1])
    wc = (wc_ref[0], wc_ref[1])

    noct = FT2 // F_OCT
    for j in range(noct):
        cols = [x_ref[pl.ds(TC + F_OCT * j + q, FT1, stride=FT2), :] for q in range(F_OCT)]
        y = _dot_hl(w1, jnp.concatenate(cols, axis=1))
        for q in range(F_OCT):
            y1_ref[q, j * 2 * FT1:(j + 1) * 2 * FT1, :] = y[:, q * FGD:(q + 1) * FGD]

    def gather(kk):
        return jnp.concatenate([y1_ref[q, pl.ds(kk, noct, stride=2 * FT1), :] for q in range(F_OCT)], axis=0)

    cstep = tw_ref[0]
    sstep = tw_ref[1]

    def rotate(c, s):
        return c * cstep - s * sstep, s * cstep + c * sstep

    def twiddled(k1, c, s):
        yr = gather(k1)
        yi = gather(FT1 + k1)
        return yr * c + yi * s, yi * c - yr * s

    def body(i, carry):
        c0, s0 = carry
        c1, s1 = rotate(c0, s0)
        k1 = 2 * i
        re0, im0 = twiddled(k1, c0, s0)
        re1, im1 = twiddled(k1 + 1, c1, s1)
        y2 = jnp.concatenate([jnp.concatenate([re0, re1], axis=1),
                              jnp.concatenate([im0, im1], axis=1)], axis=0)
        y3 = _dot_hl(w2, y2)
        lhs = jnp.concatenate([jnp.concatenate([y3[:FT2, :FGD], y3[FT2:, :FGD]], axis=1),
                               jnp.concatenate([y3[:FT2, FGD:], y3[FT2:, FGD:]], axis=1)], axis=0)
        res = _dot_lh(lhs, wc)
        o_ref[pl.ds(TC + k1, FT2, stride=FT1), :] = res[:FT2, :]
        o_ref[pl.ds(TC + k1 + 1, FT2, stride=FT1), :] = res[FT2:, :]
        return rotate(c1, s1)

    init = (jnp.ones((FT2, FGD), F32), jnp.zeros((FT2, FGD), F32))
    lax.fori_loop(0, FT1 // 2, body, init, unroll=8)

    ab = _dot_hl((wx_ref[0], wx_ref[1]), x_ref[0:TC, :])
    lhs = jnp.concatenate([ab[:TC, :], ab[TC:, :]], axis=1)
    o_ref[0:TC, :] = _dot_lh(lhs, (wcx_ref[0], wcx_ref[1]))


def _fourier(f_cat):
    w1, w2, wc, tw, wx, wcx = _fourier_tables()
    full = lambda a: pl.BlockSpec(a.shape, lambda b, g: (0,) * a.ndim)
    blk = pl.BlockSpec((None, TT, FGD), lambda b, g: (b, 0, g))
    return pl.pallas_call(
        _fourier_kernel,
        grid=(B, FG),
        in_specs=[blk, full(w1), full(w2), full(wc), full(tw), full(wx), full(wcx)],
        out_specs=blk,
        out_shape=jax.ShapeDtypeStruct((B, TT, FW), F32),
        scratch_shapes=[pltpu.VMEM((F_OCT, FT2 // F_OCT * 2 * FT1, FGD), F32)],
        compiler_params=_cp(("parallel", "parallel")),
        name="fourier_mix",
    )(f_cat, w1, w2, wc, tw, wx, wcx)


def _even_out_kernel(*refs):
    c_ref, l_refs = refs[0], refs[1:1 + NSB]
    (fo_ref, yf_ref, yb_ref, xs_ref, z_ref, gn_ref, ds_ref, wf_ref, wy_ref, g_ref, modc_ref, modl_ref,
     o_ref) = refs[1 + NSB:]
    y = yf_ref[...].astype(F32) + yb_ref[...].astype(F32) + ds_ref[...] * xs_ref[...].astype(F32)
    gated = y * _silu(z_ref[...].astype(F32))
    gn = gn_ref[...]
    parts = []
    for g in range(SSD_G):
        parts.append(_rms(gated[:, g * GW:(g + 1) * GW], gn[:, g * GW:(g + 1) * GW]))
    yn = jnp.concatenate(parts, axis=1).astype(BF16)
    o = _dot(fo_ref[...].astype(BF16), wf_ref[...]) + _dot(yn, wy_ref[...])
    gate = _row_mods(modc_ref, modl_ref, TM_CAT, True)(2)
    o_ref[...] = _stream_tile(c_ref, l_refs) + gate * _rms(o, g_ref[...])


def _even_out(ctx, x, fo, yf, yb, xbc_act, z, gn, dsum, wf, wy, g, modtab):
    row = lambda n: pl.BlockSpec((None, TM_CAT, n), lambda b, i: (b, i, 0))
    full = lambda a: pl.BlockSpec(a.shape, lambda b, i: (0,) * a.ndim)
    return pl.pallas_call(
        _even_out_kernel,
        grid=(B, TT // TM_CAT),
        in_specs=_stream_specs() + [row(FW), row(SSD_INNER), row(SSD_INNER),
                                    row(SSD_INNER), row(SSD_INNER), full(gn), full(dsum),
                                    _resident(wf), _resident(wy), full(g)] + _mod_specs(),
        out_specs=row(D),
        out_shape=jax.ShapeDtypeStruct((B, TT, D), F32),
        compiler_params=_cp(("parallel", "parallel"), 58),
        name="even_out_proj",
    )(ctx, *([x] * NSB), fo, yf, yb, xbc_act, z, gn, dsum, wf, wy, g, modtab, modtab)


def _ffn_kernel(s_ref, g2_ref, g3_ref, modc_ref, modl_ref, wg_ref, wu_ref, wd_ref, o_ref, *, cat):
    x = s_ref[...]
    mod = _row_mods(modc_ref, modl_ref, x.shape[0], cat)
    a = (_rms(x, g2_ref[...]) * (1.0 + mod(4)) + mod(3)).astype(BF16)
    gt = _dot(a, wg_ref[...])
    up = _dot(a, wu_ref[...])
    hid = (_silu(gt) * up).astype(BF16)
    f = _dot(hid, wd_ref[...])
    o_ref[...] = x + mod(5) * _rms(f, g3_ref[...])


def _ffn(s, g2, g3, modtab, wg, wu, wd, cat):
    tm = TM_FFN_CAT if cat else TM_LAT
    rows = TT if cat else T
    row = pl.BlockSpec((None, tm, D), lambda b, i: (b, i, 0))
    full = lambda a: pl.BlockSpec(a.shape, lambda b, i: (0,) * a.ndim)
    mod_specs = _mod_specs()
    mixed = cat and tm != TC
    if cat and not mixed:
        mod_specs[1] = pl.BlockSpec((None, None, 8, D), lambda b, i: (b, jnp.minimum(i, 1), 0, 0))
    return pl.pallas_call(
        functools.partial(_ffn_kernel, cat=mixed),
        grid=(B, rows // tm),
        in_specs=[row, full(g2), full(g3)] + mod_specs + [_resident(wg), _resident(wu), _resident(wd)],
        out_specs=row,
        out_shape=jax.ShapeDtypeStruct((B, rows, D), F32),
        compiler_params=_cp(("parallel", "parallel"), 60),
        name="ffn_cat" if cat else "ffn_lat",
    )(s, g2, g3, modtab, modtab, wg, wu, wd)


def _rope_tables():
    rows = T // GRID_W
    nf = QK // 4
    inv_freq = ROPE_BASE ** (-jnp.arange(nf, dtype=F32) * 2.0 / (QK // 2))
    lane = np.arange(128)
    axis = (lane % QK) // (QK // 2)
    half = (lane % (QK // 2)) // nf
    freq = lane % nf
    ang_r = (jnp.arange(rows, dtype=F32)[:, None] * inv_freq)[:, freq]
    ang_c = (jnp.arange(GRID_W, dtype=F32)[:, None] * inv_freq)[:, freq]
    by_row = jnp.asarray(axis == 0)[None, None, :]

    def per_token(fn):
        tab = jnp.where(by_row, fn(ang_r)[:, None, :], fn(ang_c)[None, :, :])
        return tab.reshape(T, 128)

    cos_l = per_token(jnp.cos)
    sin_l = per_token(jnp.sin)
    sa = jnp.where(jnp.asarray(half == 0)[None, :], -sin_l, 0.0)
    sb = jnp.where(jnp.asarray(half == 1)[None, :], sin_l, 0.0)
    ctx1 = jnp.ones((TC, 128), F32)
    ctx0 = jnp.zeros((TC, 128), F32)
    return (jnp.concatenate([ctx1, cos_l], axis=0), jnp.concatenate([ctx0, sa], axis=0),
            jnp.concatenate([ctx0, sb], axis=0))


def _odd_in_kernel(s_ref, g_ref, modc_ref, modl_ref, wq_ref, wk_ref, wv_ref, wu_ref, cos_ref, sa_ref, sb_ref,
                   q_ref, k_ref, v_ref, u_ref):
    x = s_ref[...]
    mod = _row_mods(modc_ref, modl_ref, x.shape[0], True)
    a = (_rms(x, g_ref[...]) * (1.0 + mod(1)) + mod(0)).astype(BF16)
    cos = cos_ref[...]
    sa = sa_ref[...]
    sb = sb_ref[...]

    def rope(x, scale):
        outs = []
        for h in range(NH):
            xh = x[:, h * 128:(h + 1) * 128]
            r = xh * cos + pltpu.roll(xh, 112, 1) * sa + pltpu.roll(xh, 16, 1) * sb
            outs.append(r * scale if scale != 1.0 else r)
        return jnp.concatenate(outs, axis=1)

    q_ref[...] = rope(_dot(a, wq_ref[...]), ATTN_SCALE * math.log2(math.e)).astype(BF16)
    k_ref[...] = rope(_dot(a, wk_ref[...]), 1.0).astype(BF16)
    v_ref[...] = _dot(a, wv_ref[...]).astype(BF16)
    u_ref[...] = _dot(a, wu_ref[...]).astype(BF16)


def _odd_in(s, g, modtab, wq, wk, wv, wu, cos, sa, sb):
    row = lambda n: pl.BlockSpec((None, TM_CAT, n), lambda b, i: (b, i, 0))
    full = lambda a: pl.BlockSpec(a.shape, lambda b, i: (0,) * a.ndim)
    tab = pl.BlockSpec((TM_CAT, 128), lambda b, i: (i, 0))
    return pl.pallas_call(
        _odd_in_kernel,
        grid=(B, TT // TM_CAT),
        in_specs=[row(D), full(g)] + _mod_specs() + [
                  _resident(wq), _resident(wk), _resident(wv), _resident(wu), tab, tab, tab],
        out_specs=[row(D), row(D), row(D), row(D)],
        out_shape=[jax.ShapeDtypeStruct((B, TT, D), BF16)] * 4,
        compiler_params=_cp(("parallel", "parallel"), 56),
        name="odd_in_proj",
    )(s, g, modtab, modtab, wq, wk, wv, wu, cos, sa, sb)


def _attn_kernel(*refs, lambda_init):
    nqb = TQ // TM
    q_refs = refs[:3 * nqb]
    k_ref, v_ref, lam_ref, sg_ref, o_ref, q2_ref, s_ref, mp_ref, mb_ref, ls_ref, acc_ref = refs[3 * nqb:]
    i = pl.program_id(2)
    lane_lo = lax.broadcasted_iota(jnp.int32, (TM, 128), 1) < QK

    for slot in range(3):
        for j in range(nqb):
            q = q_refs[slot * nqb + j][...]
            zero = jnp.zeros_like(q)
            q2_ref[slot, j * TM:(j + 1) * TM, :] = jnp.where(lane_lo, q, zero)
            q2_ref[slot, TQ + j * TM:TQ + (j + 1) * TM, :] = jnp.where(lane_lo, zero, q)

    def produce(c, qslot):
        r0 = c * KT if isinstance(c, int) else pl.multiple_of(c * KT, KT)
        s = _dot_nt(q2_ref[qslot], k_ref[pl.ds(r0, KT), :])
        s_ref[c] = s
        mp_ref[...] = jnp.maximum(mp_ref[...], jnp.maximum(s[:, 0:128], s[:, 128:256]))

    def finish_max():
        m = jnp.max(mp_ref[...], axis=-1, keepdims=True)
        mb_ref[...] = jnp.broadcast_to(m, mb_ref.shape)
        mp_ref[...] = jnp.full(mp_ref.shape, -jnp.inf, F32)

    def consume(c, first=False):
        r0 = c * KT if isinstance(c, int) else pl.multiple_of(c * KT, KT)
        s = s_ref[c]
        mb = mb_ref[...]
        p0 = jnp.exp2(s[:, 0:128] - mb)
        p1 = jnp.exp2(s[:, 128:256] - mb)
        p = jnp.concatenate([p0, p1], axis=1).astype(BF16)
        pv = _dot(p, v_ref[pl.ds(r0, KT), :])
        if first:
            ls_ref[...] = p0 + p1
            acc_ref[...] = pv
        else:
            ls_ref[...] += p0 + p1
            acc_ref[...] += pv

    lf = lam_ref[...]
    lam = (jnp.exp(jnp.sum(lf[0:1, :] * lf[1:2, :], axis=-1, keepdims=True))
           - jnp.exp(jnp.sum(lf[2:3, :] * lf[3:4, :], axis=-1, keepdims=True)) + lambda_init)

    def phase(next_q, out_rows):
        consume(0, first=True)
        produce(0, next_q)

        def body(c, carry):
            consume(c)
            produce(c, next_q)
            return carry

        lax.fori_loop(1, NKT, body, 0, unroll=(NKT - 1) // 2)
        finish_max()
        o = acc_ref[...] / jnp.sum(ls_ref[...], axis=-1, keepdims=True)
        od = o[0:TQ, :] - lam * o[TQ:2 * TQ, :]
        o_ref[out_rows, :] = (_rms(od, sg_ref[...]) * (1.0 - lambda_init)).astype(BF16)

    @pl.when(i == 0)
    def _():
        mp_ref[...] = jnp.full(mp_ref.shape, -jnp.inf, F32)

        def body(c, carry):
            produce(c, 0)
            return carry

        lax.fori_loop(0, NKT, body, 0, unroll=ATT_UNROLL)
        finish_max()

    phase(1, slice(0, TQ))
    phase(2, slice(TQ, 2 * TQ))


def _attention(q, k, v, lam, subln_g, lambda_init):
    nqb = TQ // TM
    nsteps = T // (2 * TQ)
    qspec = lambda j: pl.BlockSpec((None, TM, 128),
                                   lambda b, h, i: (b, jnp.minimum(2 * nqb * i + 1 + j, NT_LAT), h))
    return pl.pallas_call(
        functools.partial(_attn_kernel, lambda_init=lambda_init),
        grid=(B, NH, nsteps),
        in_specs=[qspec(j) for j in range(3 * nqb)] + [
                  pl.BlockSpec((None, TT, 128), lambda b, h, i: (b, 0, h)),
                  pl.BlockSpec((None, TT, 128), lambda b, h, i: (b, 0, h)),
                  pl.BlockSpec((4, QK), lambda b, h, i: (0, 0)),
                  pl.BlockSpec((1, DV), lambda b, h, i: (0, 0))],
        out_specs=pl.BlockSpec((None, 2 * TQ, 128), lambda b, h, i: (b, i, h)),
        out_shape=jax.ShapeDtypeStruct((B, T, NH * DV), BF16),
        scratch_shapes=[pltpu.VMEM((3, 2 * TQ, 128), BF16),
                        pltpu.VMEM((NKT, 2 * TQ, KT), F32),
                        pltpu.VMEM((2 * TQ, 128), F32),
                        pltpu.VMEM((2 * TQ, 128), F32),
                        pltpu.VMEM((2 * TQ, 128), F32),
                        pltpu.VMEM((2 * TQ, DV), F32)],
        compiler_params=_cp(("parallel", "parallel", "arbitrary"), 58),
        name="diff_attention",
    )(*([q] * (3 * nqb)), k, v, lam, subln_g)


CONF_HALO = 16


def _conf_conv_kernel(a_ref, gt_ref, w_ref, b_ref, o_ref, vs_ref, sh_ref):
    def fill(c, carry):
        r0 = pl.multiple_of(c * TM, TM)
        v = a_ref[pl.ds(r0, TM), :].astype(F32) * _sigmoid(gt_ref[pl.ds(r0, TM), :].astype(F32))
        vs_ref[pl.ds(r0, TM), :] = jnp.where(c >= 1, v, 0.0)
        return carry

    lax.fori_loop(0, NT_CAT, fill, 0)
    vs_ref[TT:TT + CONF_HALO, :] = jnp.zeros((CONF_HALO, 128), F32)
    w = w_ref[...]
    bias = b_ref[...]

    def body(c, carry):
        start = pl.multiple_of(c * TM + (TC - CONF_HALO), 8)
        win = vs_ref[pl.ds(start, TM + 2 * CONF_HALO), :]
        acc = jnp.broadcast_to(bias, (TM, 128))
        for s in range(8):
            sh_ref[...] = win[s:s + TM + CONF_HALO + 8, :]
            for k in range(CONF_K):
                off = CONF_HALO - CONF_K // 2 + k
                if off % 8 == s:
                    acc = acc + w[k:k + 1, :] * sh_ref[off - s:off - s + TM, :]
        o_ref[pl.ds(pl.multiple_of(c * TM, TM), TM), :] = acc
        return carry

    lax.fori_loop(0, NT_LAT, body, 0)


def _conf_conv(u, w32, bias):
    nct = CONF_C // 128
    return pl.pallas_call(
        _conf_conv_kernel,
        grid=(B, nct),
        in_specs=[pl.BlockSpec((None, TT, 128), lambda b, c: (b, 0, c)),
                  pl.BlockSpec((None, TT, 128), lambda b, c: (b, 0, nct + c)),
                  pl.BlockSpec((32, 128), lambda b, c: (0, c)),
                  pl.BlockSpec((1, 128), lambda b, c: (0, c))],
        out_specs=pl.BlockSpec((None, T, 128), lambda b, c: (b, 0, c)),
        out_shape=jax.ShapeDtypeStruct((B, T, CONF_C), F32),
        scratch_shapes=[pltpu.VMEM((TT + CONF_HALO, 128), F32),
                        pltpu.VMEM((TM + CONF_HALO + 8, 128), F32)],
        compiler_params=_cp(("parallel", "parallel")),
        name="conformer_conv31",
    )(u, u, w32, bias)


def _odd_out_kernel(s0_ref, s1_ref, oa_ref, cv_ref, cg_ref, cb_ref, wa_ref, wc_ref, g_ref, mod_ref, o_ref):
    v = cv_ref[...]
    mu = jnp.mean(v, axis=-1, keepdims=True)
    vc = v - mu
    var = jnp.mean(vc * vc, axis=-1, keepdims=True)
    cv = _silu(vc * lax.rsqrt(var + EPS) * cg_ref[...] + cb_ref[...])
    o = _dot(oa_ref[...], wa_ref[...]) + _dot(cv.astype(BF16), wc_ref[...])
    resid = jnp.concatenate([s0_ref[...], s1_ref[...]], axis=0)
    o_ref[...] = resid + mod_ref[2:3, :] * _rms(o, g_ref[...])


def _odd_out(s_cat, oa, cvpre, cg, cb, wa, wc, g, modtab):
    full = lambda a: pl.BlockSpec(a.shape, lambda b, i: (0,) * a.ndim)
    row = lambda n: pl.BlockSpec((None, TM_LAT, n), lambda b, i: (b, i, 0))
    assert TM_LAT == 2 * TM
    return pl.pallas_call(
        _odd_out_kernel,
        grid=(B, T // TM_LAT),
        in_specs=[pl.BlockSpec((None, TM, D), lambda b, i: (b, 2 * i + 1, 0)),
                  pl.BlockSpec((None, TM, D), lambda b, i: (b, 2 * i + 2, 0)),
                  row(NH * DV), row(CONF_C),
                  full(cg), full(cb), full(wa), full(wc), full(g),
                  pl.BlockSpec((None, None, 8, D), _lat_mod_idx)],
        out_specs=row(D),
        out_shape=jax.ShapeDtypeStruct((B, T, D), F32),
        compiler_params=_cp(("parallel", "parallel")),
        name="odd_out_proj",
    )(s_cat, s_cat, oa, cvpre, cg, cb, wa, wc, g, modtab)


def _pad_rows(a, n):
    return jnp.pad(a, ((0, n - a.shape[0]), (0, 0)))


def _pad_lanes(a, n):
    return jnp.pad(a, ((0, 0), (0, n - a.shape[1])))


def kernel(x, c, ctx, c_ctx, mod_w, mod_b, norm_g, ffn_w_gate, ffn_w_up, ffn_w_down, ev_w_in, ev_conv_w, ev_conv_b, ev_dt_bias, ev_a_log, ev_d_skip, ev_gnorm_g, ev_w_out, od_w_in, od_lambda, od_subln_g, od_conv_w, od_conv_b, od_cnorm_g, od_cnorm_b, od_w_out):
    bf = lambda a: a.astype(BF16)

    cvec = _pad_rows(jnp.concatenate([c, c_ctx[None, :]], axis=0), 8)
    mv = _mod_vectors(cvec, mod_w, mod_b).reshape(DEPTH, 8, N_MOD, D)
    modtabs = []
    for l in range(DEPTH):
        lat = mv[l, :B]
        cx = jnp.broadcast_to(mv[l, B][None], (B, N_MOD, D))
        tab = jnp.stack([cx, lat], axis=1)
        modtabs.append(jnp.pad(tab, ((0, 0), (0, 0), (0, 8 - N_MOD), (0, 0))))

    g = norm_g[0]
    w_in = ev_w_in[0]
    o0, o1, o2 = FW, FW + SSD_INNER, FW + SSD_INNER + SSD_XBC
    f, z, xbc, dt = _even_in(ctx, x, g[0:1], modtabs[0], bf(w_in[:, :o0]), bf(w_in[:, o0:o1]),
                             bf(w_in[:, o1:o2]), bf(_pad_lanes(w_in[:, o2:], DT_PAD)))
    xbc_act = _conv5(xbc, _pad_rows(ev_conv_w[0], 8), ev_conv_b[0][None, :])
    bias_all = _pad_lanes(ev_dt_bias[0].reshape(1, -1), DT_PAD)
    alog_all = _pad_lanes(ev_a_log[0].reshape(1, -1), DT_PAD)
    yf, yb = _ssd(xbc_act, dt, bias_all, alog_all)
    fo = _fourier(f)
    dsum = jnp.repeat(ev_d_skip[0, 0] + ev_d_skip[0, 1], SSD_P)[None, :]
    w_out = ev_w_out[0]
    s = _even_out(ctx, x, fo, yf, yb, xbc_act, z, ev_gnorm_g[0][None, :], dsum, bf(w_out[:FW]), bf(w_out[FW:]),
                  g[1:2], modtabs[0])
    s = _ffn(s, g[2:3], g[3:4], modtabs[0], bf(ffn_w_gate[0]), bf(ffn_w_up[0]), bf(ffn_w_down[0]), True)

    g = norm_g[1]
    lambda_init = 0.8 - 0.6 * math.exp(-0.3 * 1)
    w_in = od_w_in[0]
    cos, sa, sb = _rope_tables()
    q, k, v, u = _odd_in(s, g[0:1], modtabs[1], bf(w_in[:, :D]), bf(w_in[:, D:2 * D]),
                         bf(w_in[:, 2 * D:3 * D]), bf(w_in[:, 3 * D:]), cos, sa, sb)
    oa = _attention(q, k, v, od_lambda[0], od_subln_g[0][None, :], lambda_init)
    cvpre = _conf_conv(u, _pad_rows(od_conv_w[0], 32), od_conv_b[0][None, :])
    w_out = od_w_out[0]
    h = _odd_out(s, oa, cvpre, od_cnorm_g[0][None, :], od_cnorm_b[0][None, :], bf(w_out[:NH * DV]),
                 bf(w_out[NH * DV:]), g[1:2], modtabs[1])
    return _ffn(h, g[2:3], g[3:4], modtabs[1], bf(ffn_w_gate[1]), bf(ffn_w_up[1]), bf(ffn_w_down[1]), False)
```

```python
import functools
import math

import numpy as np
import jax
import jax.numpy as jnp
from jax import lax
from jax.experimental import pallas as pl
from jax.experimental.pallas import tpu as pltpu

F32 = jnp.float32
BF16 = jnp.bfloat16
HIGHEST = lax.Precision.HIGHEST

D = 1024
B = 2
T = 8192
TC = 256
TT = TC + T
DEPTH = 2
N_MOD = 6
HID = 2816
EPS = 1e-6
GRID_W = 64

TM = 256
NT_CAT = TT // TM
NT_LAT = T // TM

FW = 512
FG = 4
FGD = 128
SSD_INNER = 1536
SSD_G = 4
SSD_HPG = 6
SSD_P = 64
SSD_N = 128
SSD_L = 128
SSD_XBC = 2560
SSD_K = 5
NCH_CAT = TT // SSD_L
NCH_CTX = TC // SSD_L
GW = SSD_HPG * SSD_P
DT_PAD = 128

NH = 8
QK = 64
DV = 128
ATTN_SCALE = QK ** -0.5
CONF_C = 512
CONF_K = 31
ROPE_BASE = 10000.0
TQ = 512
KT = 256
NKT = TT // KT
ATT_UNROLL = 11

FT1 = 64
FT2 = 128


def _cp(sem, vmem_mb=48):
    return pltpu.CompilerParams(dimension_semantics=sem, vmem_limit_bytes=vmem_mb * 1024 * 1024)


def _sigmoid(x):
    return 1.0 / (1.0 + jnp.exp(-x))


def _silu(x):
    return x * _sigmoid(x)


def _rms(x, g):
    ms = jnp.mean(x * x, axis=-1, keepdims=True)
    return x * lax.rsqrt(ms + EPS) * g


def _dot(a, b):
    return jnp.dot(a, b, preferred_element_type=F32)


def _dot_nt(a, b):
    return lax.dot_general(a, b, (((1,), (1,)), ((), ())), preferred_element_type=F32)


def _split3(x):
    hi = x.astype(BF16)
    r1 = x - hi.astype(F32)
    mid = r1.astype(BF16)
    lo = (r1 - mid.astype(F32)).astype(BF16)
    return hi, mid, lo


def _dot3_l(m01, x):
    hi, mid, lo = _split3(x)
    return _dot(m01, hi) + _dot(m01, mid) + _dot(m01, lo)


def _dot3_r(x, m01):
    hi, mid, lo = _split3(x)
    return _dot(hi, m01) + _dot(mid, m01) + _dot(lo, m01)


def _mod_kernel(c_ref, w_ref, b_ref, o_ref):
    a = _silu(c_ref[...])
    o_ref[...] = jnp.dot(a, w_ref[...], precision=HIGHEST, preferred_element_type=F32) + b_ref[...]


def _mod_vectors(cvec, mod_w, mod_b):
    tn = 1536
    return pl.pallas_call(
        _mod_kernel,
        grid=(DEPTH, N_MOD * D // tn),
        in_specs=[
            pl.BlockSpec((8, D), lambda l, j: (0, 0)),
            pl.BlockSpec((None, D, tn), lambda l, j: (l, 0, j)),
            pl.BlockSpec((None, 1, tn), lambda l, j: (l, 0, j)),
        ],
        out_specs=pl.BlockSpec((None, 8, tn), lambda l, j: (l, 0, j)),
        out_shape=jax.ShapeDtypeStruct((DEPTH, 8, N_MOD * D), F32),
        compiler_params=_cp(("parallel", "parallel")),
        name="mod_vectors",
    )(cvec, mod_w, mod_b.reshape(DEPTH, 1, N_MOD * D))


def _lat_mod_idx(b, i):
    return (b, 1, 0, 0)


TM_CAT = 768
TM_LAT = 512
TM_FFN_CAT = 256


def _row_mods(modc_ref, modl_ref, tm, cat):
    if not cat:
        return lambda r: modl_ref[r:r + 1, :]
    rows = lax.broadcasted_iota(jnp.int32, (tm, 1), 0) + pl.program_id(1) * tm
    is_ctx = rows < TC
    return lambda r: jnp.where(is_ctx, modc_ref[r:r + 1, :], modl_ref[r:r + 1, :])


def _mod_specs():
    return [pl.BlockSpec((None, None, 8, D), lambda b, i: (b, 0, 0, 0)),
            pl.BlockSpec((None, None, 8, D), lambda b, i: (b, 1, 0, 0))]


def _resident(a):
    return pl.BlockSpec(a.shape, lambda b, i: (0,) * a.ndim, pipeline_mode=pl.Buffered(1))


NSB = TM_CAT // TM


def _stream_tile(ctx_ref, lat_refs):
    first = jnp.where(pl.program_id(1) == 0, ctx_ref[...], lat_refs[0][...])
    return jnp.concatenate([first] + [r[...] for r in lat_refs[1:]], axis=0)


def _stream_specs():
    lat = lambda j: pl.BlockSpec((None, TM, D), lambda b, i: (b, jnp.maximum(NSB * i - 1 + j, 0), 0))
    return [pl.BlockSpec((None, TM, D), lambda b, i: (b, 0, 0))] + [lat(j) for j in range(NSB)]


def _even_in_kernel(*refs):
    c_ref, l_refs = refs[0], refs[1:1 + NSB]
    g_ref, modc_ref, modl_ref, wf_ref, wz_ref, wx_ref, wd_ref, f_ref, z_ref, x_ref, dt_ref = refs[1 + NSB:]
    x = _stream_tile(c_ref, l_refs)
    mod = _row_mods(modc_ref, modl_ref, TM_CAT, True)
    a = (_rms(x, g_ref[...]) * (1.0 + mod(1)) + mod(0)).astype(BF16)
    f_ref[...] = _dot(a, wf_ref[...])
    z_ref[...] = _dot(a, wz_ref[...]).astype(BF16)
    x_ref[...] = _dot(a, wx_ref[...]).astype(BF16)
    dt_ref[...] = _dot(a, wd_ref[...])


def _even_in(ctx, x, g, modtab, wf, wz, wx, wd):
    row = lambda n: pl.BlockSpec((None, TM_CAT, n), lambda b, i: (b, i, 0))
    full = lambda a: pl.BlockSpec(a.shape, lambda b, i: (0,) * a.ndim)
    return pl.pallas_call(
        _even_in_kernel,
        grid=(B, TT // TM_CAT),
        in_specs=_stream_specs() + [full(g)] + _mod_specs() + [
            _resident(wf), _resident(wz), _resident(wx), _resident(wd)],
        out_specs=[row(FW), row(SSD_INNER), row(SSD_XBC), row(DT_PAD)],
        out_shape=[jax.ShapeDtypeStruct((B, TT, n), dt) for n, dt in
                   ((FW, F32), (SSD_INNER, BF16), (SSD_XBC, BF16), (DT_PAD, F32))],
        compiler_params=_cp(("parallel", "parallel"), 58),
        name="even_in_proj",
    )(ctx, *([x] * NSB), g, modtab, modtab, wf, wz, wx, wd)


CONV_HALO = 16


def _conv5_kernel(x_ref, w_ref, b_ref, o_ref):
    w = w_ref[...]
    bias = b_ref[...]

    def body(c, carry):
        r0 = pl.multiple_of(c * TM, TM)
        center = x_ref[pl.ds(r0, TM), :].astype(F32)
        lo = pl.multiple_of(jnp.maximum(r0 - CONV_HALO, 0), CONV_HALO)
        hi = pl.multiple_of(jnp.minimum(r0 + TM, TT - CONV_HALO), CONV_HALO)
        left = jnp.where(c >= 2, x_ref[pl.ds(lo, CONV_HALO), :].astype(F32), 0.0)
        right = jnp.where(jnp.logical_and(c >= 1, c < NT_CAT - 1),
                          x_ref[pl.ds(hi, CONV_HALO), :].astype(F32), 0.0)
        ext = jnp.concatenate([left, center, right], axis=0)
        acc = bias + w[2:3, :] * center
        for k in (0, 1, 3, 4):
            off = CONV_HALO - SSD_K // 2 + k
            acc = acc + w[k:k + 1, :] * ext[off:off + TM, :]
        o_ref[pl.ds(r0, TM), :] = _silu(acc).astype(BF16)
        return carry

    lax.fori_loop(0, NT_CAT, body, 0, unroll=3)


def _conv5(xbc, w8, bias):
    return pl.pallas_call(
        _conv5_kernel,
        grid=(B, SSD_XBC // 128),
        in_specs=[
            pl.BlockSpec((None, TT, 128), lambda b, c: (b, 0, c)),
            pl.BlockSpec((8, 128), lambda b, c: (0, c)),
            pl.BlockSpec((1, 128), lambda b, c: (0, c)),
        ],
        out_specs=pl.BlockSpec((None, TT, 128), lambda b, c: (b, 0, c)),
        out_shape=jax.ShapeDtypeStruct((B, TT, SSD_XBC), BF16),
        compiler_params=_cp(("parallel", "parallel")),
        name="ssd_conv5",
    )(xbc, w8, bias)


def _ssd_bwd_chunk(i):
    return jnp.where(i < NCH_CTX, NCH_CTX - 1 - i, NCH_CAT + NCH_CTX - 1 - i)


SSD_NH = SSD_G * SSD_HPG


def _ssd_expand_table():
    lane = np.arange(DT_PAD)[:, None]
    col = np.arange(SSD_INNER)[None, :]
    tabs = []
    for d in range(2):
        e = (lane == d * SSD_NH + col // SSD_P).astype(np.float32)
        tabs.append(np.concatenate([e, e], axis=0))
    return jnp.asarray(np.stack(tabs), dtype=BF16)


def _hi_mid(x):
    hi = x.astype(BF16)
    mid = (x - hi.astype(F32)).astype(BF16)
    return jnp.concatenate([hi, mid], axis=1)


def _ssd_kernel(xf_ref, bf_ref, cf_ref, dtf_ref, xb_ref, bb_ref, cb_ref, dtb_ref, bias_ref, alog_ref,
                ep_ref, yf_ref, yb_ref, st_ref):
    @pl.when(pl.program_id(1) == 0)
    def _():
        st_ref[...] = jnp.zeros_like(st_ref)

    row = lax.broadcasted_iota(jnp.int32, (SSD_L, SSD_L), 0)
    col = lax.broadcasted_iota(jnp.int32, (SSD_L, SSD_L), 1)
    lane_lo = lax.broadcasted_iota(jnp.int32, (SSD_L, 128), 1) < SSD_P
    bias = bias_ref[...]
    a_all = -jnp.exp(alog_ref[...]) * math.log2(math.e)
    dirs = ((xf_ref, bf_ref, cf_ref, dtf_ref, yf_ref), (xb_ref, bb_ref, cb_ref, dtb_ref, yb_ref))

    for d, (xs_ref, bm_ref, cm_ref, dt_ref, y_ref) in enumerate(dirs):
        tri = (row >= col) if d == 0 else (row <= col)
        tri_b = jnp.where(tri, 1.0, 0.0).astype(BF16)
        raw = dt_ref[...] + bias
        dt_all = jnp.maximum(raw, 0.0) + jnp.log1p(jnp.exp(-jnp.abs(raw)))
        acs = _dot3_l(tri_b, dt_all * a_all)
        acs_t = acs.T
        dt_e = _dot(_hi_mid(dt_all), ep_ref[d])
        acs_e = _dot(_hi_mid(acs), ep_ref[d])
        last = SSD_L - 1 if d == 0 else 0
        tot_e = acs_e[last:last + 1, :]

        def head_mat(cb, h):
            c = d * SSD_NH + h
            seg = acs[:, c:c + 1] - acs_t[c:c + 1, :]
            dec = jnp.exp2(jnp.where(tri, seg, -jnp.inf))
            return (cb * dec).astype(BF16)

        for g in range(SSD_G):
            cm_b = cm_ref[:, g * SSD_N:(g + 1) * SSD_N]
            bm = bm_ref[:, g * SSD_N:(g + 1) * SSD_N]
            cb = _dot_nt(cm_b, bm)
            bm_t = bm.astype(F32).T.astype(BF16)
            glanes = slice(g * GW, (g + 1) * GW)
            acs_g = acs_e[:, glanes]
            tot_g = tot_e[:, glanes]
            xdt = xs_ref[:, glanes].astype(F32) * dt_e[:, glanes]
            st_old = st_ref[d, g]
            y_off = _dot(cm_b, st_old.astype(BF16)) * jnp.exp2(acs_g)
            st_ref[d, g] = st_old * jnp.exp2(tot_g) + _dot(bm_t, (xdt * jnp.exp2(tot_g - acs_g)).astype(BF16))
            for j in range(SSD_HPG // 2):
                h0 = g * SSD_HPG + 2 * j
                x_p = xdt[:, j * 128:(j + 1) * 128]
                x2 = jnp.concatenate([jnp.where(lane_lo, x_p, 0.0), jnp.where(lane_lo, 0.0, x_p)], axis=0)
                m2 = jnp.concatenate([head_mat(cb, h0), head_mat(cb, h0 + 1)], axis=1)
                y_diag = _dot(m2, x2.astype(BF16))
                lanes = slice(g * GW + j * 128, g * GW + (j + 1) * 128)
                y_ref[:, lanes] = (y_diag + y_off[:, j * 128:(j + 1) * 128]).astype(BF16)


def _ssd(xbc_act, dt, bias_all, alog_all):
    nbc = SSD_INNER // (SSD_G * SSD_N)
    ep = _ssd_expand_table()
    fwd = lambda blk: (lambda b, i: (b, i, blk))
    bwd = lambda blk: (lambda b, i: (b, _ssd_bwd_chunk(i), blk))
    specs = lambda ix: [pl.BlockSpec((None, SSD_L, SSD_INNER), ix(0)),
                        pl.BlockSpec((None, SSD_L, SSD_G * SSD_N), ix(nbc)),
                        pl.BlockSpec((None, SSD_L, SSD_G * SSD_N), ix(nbc + 1)),
                        pl.BlockSpec((None, SSD_L, DT_PAD), ix(0))]
    par = pl.BlockSpec((1, DT_PAD), lambda b, i: (0, 0))
    full = lambda a: pl.BlockSpec(a.shape, lambda b, i: (0,) * a.ndim)
    return pl.pallas_call(
        _ssd_kernel,
        grid=(B, NCH_CAT),
        in_specs=specs(fwd) + specs(bwd) + [par, par, full(ep)],
        out_specs=[pl.BlockSpec((None, SSD_L, SSD_INNER), fwd(0)),
                   pl.BlockSpec((None, SSD_L, SSD_INNER), bwd(0))],
        out_shape=[jax.ShapeDtypeStruct((B, TT, SSD_INNER), BF16)] * 2,
        scratch_shapes=[pltpu.VMEM((2, SSD_G, SSD_N, GW), F32)],
        compiler_params=_cp(("parallel", "arbitrary")),
        name="ssd_scan",
    )(xbc_act, xbc_act, xbc_act, dt, xbc_act, xbc_act, xbc_act, dt, bias_all, alog_all, ep)


def _dft_cs(n):
    k = np.arange(n, dtype=np.int64)
    ang = 2.0 * np.pi * ((k[:, None] * k[None, :]) % n).astype(np.float64) / n
    return np.cos(ang), np.sin(ang)


def _hi_lo(a):
    a32 = jnp.asarray(a, dtype=F32)
    hi = a32.astype(BF16)
    lo = (a32 - hi.astype(F32)).astype(BF16)
    return jnp.stack([hi, lo], axis=0)


def _fourier_tables():
    c1, s1 = _dft_cs(FT1)
    w1 = np.concatenate([c1, -s1], axis=0)
    r = np.arange(FT2)
    t2_of_r = F_OCT * (r % (FT2 // F_OCT)) + r // (FT2 // F_OCT)
    c2, s2 = _dft_cs(FT2)
    c2, s2 = c2[:, t2_of_r], s2[:, t2_of_r]
    w2 = np.block([[c2, s2], [-s2, c2]])
    cc, sc = _dft_cs(FGD)
    wc = np.concatenate([cc, sc], axis=0) / math.sqrt(T * FGD)
    ang = 2.0 * np.pi * t2_of_r.astype(np.float64)[:, None] / T
    tw = np.stack([np.cos(ang), np.sin(ang)], axis=0) * np.ones((1, 1, FGD))
    cx, sx = _dft_cs(TC)
    wx = np.concatenate([cx, sx], axis=0)
    wcx = np.concatenate([cc, -sc], axis=0) / math.sqrt(TC * FGD)
    return _hi_lo(w1), _hi_lo(w2), _hi_lo(wc), jnp.asarray(tw, dtype=F32), _hi_lo(wx), _hi_lo(wcx)


def _dot_hl(a_hl, x):
    xh = x.astype(BF16)
    xl = (x - xh.astype(F32)).astype(BF16)
    return _dot(a_hl[0], xh) + _dot(a_hl[0], xl) + _dot(a_hl[1], xh)


def _dot_lh(x, b_hl):
    xh = x.astype(BF16)
    xl = (x - xh.astype(F32)).astype(BF16)
    return _dot(xh, b_hl[0]) + _dot(xl, b_hl[0]) + _dot(xh, b_hl[1])


F_OCT = 8


def _fourier_kernel(x_ref, w1_ref, w2_ref, wc_ref, tw_ref, wx_ref, wcx_ref, o_ref, y1_ref):
    w1 = (w1_ref[0], w1_ref[1])
    w2 = (w2_ref[0], w2_ref[1])
    wc = (wc_ref[0], wc_ref[1])

    noct = FT2 // F_OCT
    for j in range(noct):
        cols = [x_ref[pl.ds(TC + F_OCT * j + q, FT1, stride=FT2), :] for q in range(F_OCT)]
        y = _dot_hl(w1, jnp.concatenate(cols, axis=1))
        for q in range(F_OCT):
            y1_ref[q, j * 2 * FT1:(j + 1) * 2 * FT1, :] = y[:, q * FGD:(q + 1) * FGD]

    def gather(kk):
        return jnp.concatenate([y1_ref[q, pl.ds(kk, noct, stride=2 * FT1), :] for q in range(F_OCT)], axis=0)

    cstep = tw_ref[0]
    sstep = tw_ref[1]

    def rotate(c, s):
        return c * cstep - s * sstep, s * cstep + c * sstep

    def twiddled(k1, c, s):
        yr = gather(k1)
        yi = gather(FT1 + k1)
        return yr * c + yi * s, yi * c - yr * s

    def body(i, carry):
        c0, s0 = carry
        c1, s1 = rotate(c0, s0)
        k1 = 2 * i
        re0, im0 = twiddled(k1, c0, s0)
        re1, im1 = twiddled(k1 + 1, c1, s1)
        y2 = jnp.concatenate([jnp.concatenate([re0, re1], axis=1),
                              jnp.concatenate([im0, im1], axis=1)], axis=0)
        y3 = _dot_hl(w2, y2)
        lhs = jnp.concatenate([jnp.concatenate([y3[:FT2, :FGD], y3[FT2:, :FGD]], axis=1),
                               jnp.concatenate([y3[:FT2, FGD:], y3[FT2:, FGD:]], axis=1)], axis=0)
        res = _dot_lh(lhs, wc)
        o_ref[pl.ds(TC + k1, FT2, stride=FT1), :] = res[:FT2, :]
        o_ref[pl.ds(TC + k1 + 1, FT2, stride=FT1), :] = res[FT2:, :]
        return rotate(c1, s1)

    init = (jnp.ones((FT2, FGD), F32), jnp.zeros((FT2, FGD), F32))
    lax.fori_loop(0, FT1 // 2, body, init, unroll=16)

    ab = _dot_hl((wx_ref[0], wx_ref[1]), x_ref[0:TC, :])
    lhs = jnp.concatenate([ab[:TC, :], ab[TC:, :]], axis=1)
    o_ref[0:TC, :] = _dot_lh(lhs, (wcx_ref[0], wcx_ref[1]))


def _fourier(f_cat):
    w1, w2, wc, tw, wx, wcx = _fourier_tables()
    full = lambda a: pl.BlockSpec(a.shape, lambda b, g: (0,) * a.ndim)
    blk = pl.BlockSpec((None, TT, FGD), lambda b, g: (b, 0, g))
    return pl.pallas_call(
        _fourier_kernel,
        grid=(B, FG),
        in_specs=[blk, full(w1), full(w2), full(wc), full(tw), full(wx), full(wcx)],
        out_specs=blk,
        out_shape=jax.ShapeDtypeStruct((B, TT, FW), F32),
        scratch_shapes=[pltpu.VMEM((F_OCT, FT2 // F_OCT * 2 * FT1, FGD), F32)],
        compiler_params=_cp(("parallel", "parallel")),
        name="fourier_mix",
    )(f_cat, w1, w2, wc, tw, wx, wcx)


def _even_out_kernel(*refs):
    c_ref, l_refs = refs[0], refs[1:1 + NSB]
    (fo_ref, yf_ref, yb_ref, xs_ref, z_ref, gn_ref, ds_ref, wf_ref, wy_ref, g_ref, modc_ref, modl_ref,
     o_ref) = refs[1 + NSB:]
    y = yf_ref[...].astype(F32) + yb_ref[...].astype(F32) + ds_ref[...] * xs_ref[...].astype(F32)
    gated = y * _silu(z_ref[...].astype(F32))
    gn = gn_ref[...]
    parts = []
    for g in range(SSD_G):
        parts.append(_rms(gated[:, g * GW:(g + 1) * GW], gn[:, g * GW:(g + 1) * GW]))
    yn = jnp.concatenate(parts, axis=1).astype(BF16)
    o = _dot(fo_ref[...].astype(BF16), wf_ref[...]) + _dot(yn, wy_ref[...])
    gate = _row_mods(modc_ref, modl_ref, TM_CAT, True)(2)
    o_ref[...] = _stream_tile(c_ref, l_refs) + gate * _rms(o, g_ref[...])


def _even_out(ctx, x, fo, yf, yb, xbc_act, z, gn, dsum, wf, wy, g, modtab):
    row = lambda n: pl.BlockSpec((None, TM_CAT, n), lambda b, i: (b, i, 0))
    full = lambda a: pl.BlockSpec(a.shape, lambda b, i: (0,) * a.ndim)
    return pl.pallas_call(
        _even_out_kernel,
        grid=(B, TT // TM_CAT),
        in_specs=_stream_specs() + [row(FW), row(SSD_INNER), row(SSD_INNER),
                                    row(SSD_INNER), row(SSD_INNER), full(gn), full(dsum),
                                    _resident(wf), _resident(wy), full(g)] + _mod_specs(),
        out_specs=row(D),
        out_shape=jax.ShapeDtypeStruct((B, TT, D), F32),
        compiler_params=_cp(("parallel", "parallel"), 58),
        name="even_out_proj",
    )(ctx, *([x] * NSB), fo, yf, yb, xbc_act, z, gn, dsum, wf, wy, g, modtab, modtab)


def _ffn_kernel(s_ref, g2_ref, g3_ref, modc_ref, modl_ref, wg_ref, wu_ref, wd_ref, o_ref, *, cat):
    x = s_ref[...]
    mod = _row_mods(modc_ref, modl_ref, x.shape[0], cat)
    a = (_rms(x, g2_ref[...]) * (1.0 + mod(4)) + mod(3)).astype(BF16)
    gt = _dot(a, wg_ref[...])
    up = _dot(a, wu_ref[...])
    hid = (_silu(gt) * up).astype(BF16)
    f = _dot(hid, wd_ref[...])
    o_ref[...] = x + mod(5) * _rms(f, g3_ref[...])


def _ffn(s, g2, g3, modtab, wg, wu, wd, cat):
    tm = TM_FFN_CAT if cat else TM_LAT
    rows = TT if cat else T
    row = pl.BlockSpec((None, tm, D), lambda b, i: (b, i, 0))
    full = lambda a: pl.BlockSpec(a.shape, lambda b, i: (0,) * a.ndim)
    mod_specs = _mod_specs()
    mixed = cat and tm != TC
    if cat and not mixed:
        mod_specs[1] = pl.BlockSpec((None, None, 8, D), lambda b, i: (b, jnp.minimum(i, 1), 0, 0))
    return pl.pallas_call(
        functools.partial(_ffn_kernel, cat=mixed),
        grid=(B, rows // tm),
        in_specs=[row, full(g2), full(g3)] + mod_specs + [_resident(wg), _resident(wu), _resident(wd)],
        out_specs=row,
        out_shape=jax.ShapeDtypeStruct((B, rows, D), F32),
        compiler_params=_cp(("parallel", "parallel"), 60),
        name="ffn_cat" if cat else "ffn_lat",
    )(s, g2, g3, modtab, modtab, wg, wu, wd)


def _rope_tables():
    rows = T // GRID_W
    nf = QK // 4
    inv_freq = ROPE_BASE ** (-jnp.arange(nf, dtype=F32) * 2.0 / (QK // 2))
    lane = np.arange(128)
    axis = (lane % QK) // (QK // 2)
    half = (lane % (QK // 2)) // nf
    freq = lane % nf
    ang_r = (jnp.arange(rows, dtype=F32)[:, None] * inv_freq)[:, freq]
    ang_c = (jnp.arange(GRID_W, dtype=F32)[:, None] * inv_freq)[:, freq]
    by_row = jnp.asarray(axis == 0)[None, None, :]

    def per_token(fn):
        tab = jnp.where(by_row, fn(ang_r)[:, None, :], fn(ang_c)[None, :, :])
        return tab.reshape(T, 128)

    cos_l = per_token(jnp.cos)
    sin_l = per_token(jnp.sin)
    sa = jnp.where(jnp.asarray(half == 0)[None, :], -sin_l, 0.0)
    sb = jnp.where(jnp.asarray(half == 1)[None, :], sin_l, 0.0)
    ctx1 = jnp.ones((TC, 128), F32)
    ctx0 = jnp.zeros((TC, 128), F32)
    return (jnp.concatenate([ctx1, cos_l], axis=0), jnp.concatenate([ctx0, sa], axis=0),
            jnp.concatenate([ctx0, sb], axis=0))


def _odd_in_kernel(s_ref, g_ref, modc_ref, modl_ref, wq_ref, wk_ref, wv_ref, wu_ref, cos_ref, sa_ref, sb_ref,
                   q_ref, k_ref, v_ref, u_ref):
    x = s_ref[...]
    mod = _row_mods(modc_ref, modl_ref, x.shape[0], True)
    a = (_rms(x, g_ref[...]) * (1.0 + mod(1)) + mod(0)).astype(BF16)
    cos = cos_ref[...]
    sa = sa_ref[...]
    sb = sb_ref[...]

    def rope(x, scale):
        outs = []
        for h in range(NH):
            xh = x[:, h * 128:(h + 1) * 128]
            r = xh * cos + pltpu.roll(xh, 112, 1) * sa + pltpu.roll(xh, 16, 1) * sb
            outs.append(r * scale if scale != 1.0 else r)
        return jnp.concatenate(outs, axis=1)

    q_ref[...] = rope(_dot(a, wq_ref[...]), ATTN_SCALE * math.log2(math.e)).astype(BF16)
    k_ref[...] = rope(_dot(a, wk_ref[...]), 1.0).astype(BF16)
    v_ref[...] = _dot(a, wv_ref[...]).astype(BF16)
    u_ref[...] = _dot(a, wu_ref[...]).astype(BF16)


def _odd_in(s, g, modtab, wq, wk, wv, wu, cos, sa, sb):
    row = lambda n: pl.BlockSpec((None, TM_CAT, n), lambda b, i: (b, i, 0))
    full = lambda a: pl.BlockSpec(a.shape, lambda b, i: (0,) * a.ndim)
    tab = pl.BlockSpec((TM_CAT, 128), lambda b, i: (i, 0))
    return pl.pallas_call(
        _odd_in_kernel,
        grid=(B, TT // TM_CAT),
        in_specs=[row(D), full(g)] + _mod_specs() + [
                  _resident(wq), _resident(wk), _resident(wv), _resident(wu), tab, tab, tab],
        out_specs=[row(D), row(D), row(D), row(D)],
        out_shape=[jax.ShapeDtypeStruct((B, TT, D), BF16)] * 4,
        compiler_params=_cp(("parallel", "parallel"), 56),
        name="odd_in_proj",
    )(s, g, modtab, modtab, wq, wk, wv, wu, cos, sa, sb)


def _attn_kernel(*refs, lambda_init):
    nqb = TQ // TM
    q_refs = refs[:3 * nqb]
    k_ref, v_ref, lam_ref, sg_ref, o_ref, q2_ref, s_ref, mp_ref, mb_ref, ls_ref, acc_ref = refs[3 * nqb:]
    i = pl.program_id(2)
    lane_lo = lax.broadcasted_iota(jnp.int32, (TM, 128), 1) < QK

    for slot in range(3):
        for j in range(nqb):
            q = q_refs[slot * nqb + j][...]
            zero = jnp.zeros_like(q)
            q2_ref[slot, j * TM:(j + 1) * TM, :] = jnp.where(lane_lo, q, zero)
            q2_ref[slot, TQ + j * TM:TQ + (j + 1) * TM, :] = jnp.where(lane_lo, zero, q)

    def produce(c, qslot):
        r0 = c * KT if isinstance(c, int) else pl.multiple_of(c * KT, KT)
        s = _dot_nt(q2_ref[qslot], k_ref[pl.ds(r0, KT), :])
        s_ref[c] = s
        mp_ref[...] = jnp.maximum(mp_ref[...], jnp.maximum(s[:, 0:128], s[:, 128:256]))

    def finish_max():
        m = jnp.max(mp_ref[...], axis=-1, keepdims=True)
        mb_ref[...] = jnp.broadcast_to(m, mb_ref.shape)
        mp_ref[...] = jnp.full(mp_ref.shape, -jnp.inf, F32)

    def consume(c, first=False):
        r0 = c * KT if isinstance(c, int) else pl.multiple_of(c * KT, KT)
        s = s_ref[c]
        mb = mb_ref[...]
        p0 = jnp.exp2(s[:, 0:128] - mb)
        p1 = jnp.exp2(s[:, 128:256] - mb)
        p = jnp.concatenate([p0, p1], axis=1).astype(BF16)
        pv = _dot(p, v_ref[pl.ds(r0, KT), :])
        if first:
            ls_ref[...] = p0 + p1
            acc_ref[...] = pv
        else:
            ls_ref[...] += p0 + p1
            acc_ref[...] += pv

    lf = lam_ref[...]
    lam = (jnp.exp(jnp.sum(lf[0:1, :] * lf[1:2, :], axis=-1, keepdims=True))
           - jnp.exp(jnp.sum(lf[2:3, :] * lf[3:4, :], axis=-1, keepdims=True)) + lambda_init)

    def phase(next_q, out_rows):
        consume(0, first=True)
        produce(0, next_q)

        def body(c, carry):
            consume(c)
            produce(c, next_q)
            return carry

        lax.fori_loop(1, NKT, body, 0, unroll=(NKT - 1) // 2)
        finish_max()
        o = acc_ref[...] / jnp.sum(ls_ref[...], axis=-1, keepdims=True)
        od = o[0:TQ, :] - lam * o[TQ:2 * TQ, :]
        o_ref[out_rows, :] = (_rms(od, sg_ref[...]) * (1.0 - lambda_init)).astype(BF16)

    @pl.when(i == 0)
    def _():
        mp_ref[...] = jnp.full(mp_ref.shape, -jnp.inf, F32)

        def body(c, carry):
            produce(c, 0)
            return carry

        lax.fori_loop(0, NKT, body, 0, unroll=ATT_UNROLL)
        finish_max()

    phase(1, slice(0, TQ))
    phase(2, slice(TQ, 2 * TQ))


def _attention(q, k, v, lam, subln_g, lambda_init):
    nqb = TQ // TM
    nsteps = T // (2 * TQ)
    qspec = lambda j: pl.BlockSpec((None, TM, 128),
                                   lambda b, h, i: (b, jnp.minimum(2 * nqb * i + 1 + j, NT_LAT), h))
    return pl.pallas_call(
        functools.partial(_attn_kernel, lambda_init=lambda_init),
        grid=(B, NH, nsteps),
        in_specs=[qspec(j) for j in range(3 * nqb)] + [
                  pl.BlockSpec((None, TT, 128), lambda b, h, i: (b, 0, h)),
                  pl.BlockSpec((None, TT, 128), lambda b, h, i: (b, 0, h)),
                  pl.BlockSpec((4, QK), lambda b, h, i: (0, 0)),
                  pl.BlockSpec((1, DV), lambda b, h, i: (0, 0))],
        out_specs=pl.BlockSpec((None, 2 * TQ, 128), lambda b, h, i: (b, i, h)),
        out_shape=jax.ShapeDtypeStruct((B, T, NH * DV), BF16),
        scratch_shapes=[pltpu.VMEM((3, 2 * TQ, 128), BF16),
                        pltpu.VMEM((NKT, 2 * TQ, KT), F32),
                        pltpu.VMEM((2 * TQ, 128), F32),
                        pltpu.VMEM((2 * TQ, 128), F32),
                        pltpu.VMEM((2 * TQ, 128), F32),
                        pltpu.VMEM((2 * TQ, DV), F32)],
        compiler_params=_cp(("parallel", "parallel", "arbitrary"), 58),
        name="diff_attention",
    )(*([q] * (3 * nqb)), k, v, lam, subln_g)


CONF_HALO = 16


def _conf_conv_kernel(a_ref, gt_ref, w_ref, b_ref, o_ref, vs_ref, sh_ref):
    def fill(c, carry):
        r0 = pl.multiple_of(c * TM, TM)
        v = a_ref[pl.ds(r0, TM), :].astype(F32) * _sigmoid(gt_ref[pl.ds(r0, TM), :].astype(F32))
        vs_ref[pl.ds(r0, TM), :] = jnp.where(c >= 1, v, 0.0)
        return carry

    lax.fori_loop(0, NT_CAT, fill, 0)
    vs_ref[TT:TT + CONF_HALO, :] = jnp.zeros((CONF_HALO, 128), F32)
    w = w_ref[...]
    bias = b_ref[...]

    def body(c, carry):
        start = pl.multiple_of(c * TM + (TC - CONF_HALO), 8)
        win = vs_ref[pl.ds(start, TM + 2 * CONF_HALO), :]
        acc = jnp.broadcast_to(bias, (TM, 128))
        for s in range(8):
            sh_ref[...] = win[s:s + TM + CONF_HALO + 8, :]
            for k in range(CONF_K):
                off = CONF_HALO - CONF_K // 2 + k
                if off % 8 == s:
                    acc = acc + w[k:k + 1, :] * sh_ref[off - s:off - s + TM, :]
        o_ref[pl.ds(pl.multiple_of(c * TM, TM), TM), :] = acc
        return carry

    lax.fori_loop(0, NT_LAT, body, 0, unroll=2)


def _conf_conv(u, w32, bias):
    nct = CONF_C // 128
    return pl.pallas_call(
        _conf_conv_kernel,
        grid=(B, nct),
        in_specs=[pl.BlockSpec((None, TT, 128), lambda b, c: (b, 0, c)),
                  pl.BlockSpec((None, TT, 128), lambda b, c: (b, 0, nct + c)),
                  pl.BlockSpec((32, 128), lambda b, c: (0, c)),
                  pl.BlockSpec((1, 128), lambda b, c: (0, c))],
        out_specs=pl.BlockSpec((None, T, 128), lambda b, c: (b, 0, c)),
        out_shape=jax.ShapeDtypeStruct((B, T, CONF_C), F32),
        scratch_shapes=[pltpu.VMEM((TT + CONF_HALO, 128), F32),
                        pltpu.VMEM((TM + CONF_HALO + 8, 128), F32)],
        compiler_params=_cp(("parallel", "parallel")),
        name="conformer_conv31",
    )(u, u, w32, bias)


def _odd_out_kernel(s0_ref, s1_ref, oa_ref, cv_ref, cg_ref, cb_ref, wa_ref, wc_ref, g_ref, mod_ref, o_ref):
    v = cv_ref[...]
    mu = jnp.mean(v, axis=-1, keepdims=True)
    vc = v - mu
    var = jnp.mean(vc * vc, axis=-1, keepdims=True)
    cv = _silu(vc * lax.rsqrt(var + EPS) * cg_ref[...] + cb_ref[...])
    o = _dot(oa_ref[...], wa_ref[...]) + _dot(cv.astype(BF16), wc_ref[...])
    resid = jnp.concatenate([s0_ref[...], s1_ref[...]], axis=0)
    o_ref[...] = resid + mod_ref[2:3, :] * _rms(o, g_ref[...])


def _odd_out(s_cat, oa, cvpre, cg, cb, wa, wc, g, modtab):
    full = lambda a: pl.BlockSpec(a.shape, lambda b, i: (0,) * a.ndim)
    row = lambda n: pl.BlockSpec((None, TM_LAT, n), lambda b, i: (b, i, 0))
    assert TM_LAT == 2 * TM
    return pl.pallas_call(
        _odd_out_kernel,
        grid=(B, T // TM_LAT),
        in_specs=[pl.BlockSpec((None, TM, D), lambda b, i: (b, 2 * i + 1, 0)),
                  pl.BlockSpec((None, TM, D), lambda b, i: (b, 2 * i + 2, 0)),
                  row(NH * DV), row(CONF_C),
                  full(cg), full(cb), full(wa), full(wc), full(g),
                  pl.BlockSpec((None, None, 8, D), _lat_mod_idx)],
        out_specs=row(D),
        out_shape=jax.ShapeDtypeStruct((B, T, D), F32),
        compiler_params=_cp(("parallel", "parallel")),
        name="odd_out_proj",
    )(s_cat, s_cat, oa, cvpre, cg, cb, wa, wc, g, modtab)


def _pad_rows(a, n):
    return jnp.pad(a, ((0, n - a.shape[0]), (0, 0)))


def _pad_lanes(a, n):
    return jnp.pad(a, ((0, 0), (0, n - a.shape[1])))


def kernel(x, c, ctx, c_ctx, mod_w, mod_b, norm_g, ffn_w_gate, ffn_w_up, ffn_w_down, ev_w_in, ev_conv_w, ev_conv_b, ev_dt_bias, ev_a_log, ev_d_skip, ev_gnorm_g, ev_w_out, od_w_in, od_lambda, od_subln_g, od_conv_w, od_conv_b, od_cnorm_g, od_cnorm_b, od_w_out):
    bf = lambda a: a.astype(BF16)

    cvec = _pad_rows(jnp.concatenate([c, c_ctx[None, :]], axis=0), 8)
    mv = _mod_vectors(cvec, mod_w, mod_b).reshape(DEPTH, 8, N_MOD, D)
    modtabs = []
    for l in range(DEPTH):
        lat = mv[l, :B]
        cx = jnp.broadcast_to(mv[l, B][None], (B, N_MOD, D))
        tab = jnp.stack([cx, lat], axis=1)
        modtabs.append(jnp.pad(tab, ((0, 0), (0, 0), (0, 8 - N_MOD), (0, 0))))

    g = norm_g[0]
    w_in = ev_w_in[0]
    o0, o1, o2 = FW, FW + SSD_INNER, FW + SSD_INNER + SSD_XBC
    f, z, xbc, dt = _even_in(ctx, x, g[0:1], modtabs[0], bf(w_in[:, :o0]), bf(w_in[:, o0:o1]),
                             bf(w_in[:, o1:o2]), bf(_pad_lanes(w_in[:, o2:], DT_PAD)))
    xbc_act = _conv5(xbc, _pad_rows(ev_conv_w[0], 8), ev_conv_b[0][None, :])
    bias_all = _pad_lanes(ev_dt_bias[0].reshape(1, -1), DT_PAD)
    alog_all = _pad_lanes(ev_a_log[0].reshape(1, -1), DT_PAD)
    yf, yb = _ssd(xbc_act, dt, bias_all, alog_all)
    fo = _fourier(f)
    dsum = jnp.repeat(ev_d_skip[0, 0] + ev_d_skip[0, 1], SSD_P)[None, :]
    w_out = ev_w_out[0]
    s = _even_out(ctx, x, fo, yf, yb, xbc_act, z, ev_gnorm_g[0][None, :], dsum, bf(w_out[:FW]), bf(w_out[FW:]),
                  g[1:2], modtabs[0])
    s = _ffn(s, g[2:3], g[3:4], modtabs[0], bf(ffn_w_gate[0]), bf(ffn_w_up[0]), bf(ffn_w_down[0]), True)

    g = norm_g[1]
    lambda_init = 0.8 - 0.6 * math.exp(-0.3 * 1)
    w_in = od_w_in[0]
    cos, sa, sb = _rope_tables()
    q, k, v, u = _odd_in(s, g[0:1], modtabs[1], bf(w_in[:, :D]), bf(w_in[:, D:2 * D]),
                         bf(w_in[:, 2 * D:3 * D]), bf(w_in[:, 3 * D:]), cos, sa, sb)
    oa = _attention(q, k, v, od_lambda[0], od_subln_g[0][None, :], lambda_init)
    cvpre = _conf_conv(u, _pad_rows(od_conv_w[0], 32), od_conv_b[0][None, :])
    w_out = od_w_out[0]
    h = _odd_out(s, oa, cvpre, od_cnorm_g[0][None, :], od_cnorm_b[0][None, :], bf(w_out[:NH * DV]),
                 bf(w_out[NH * DV:]), g[1:2], modtabs[1])
    return _ffn(h, g[2:3], g[3:4], modtabs[1], bf(ffn_w_gate[1]), bf(ffn_w_up[1]), bf(ffn_w_down[1]), False)
```
